```python
import math
import jax
import jax.numpy as jnp
from jax import lax
import numpy as np

D_MODEL = 2048
BATCH = 8
SEQ = 4096
DEPTH = 1

CHUNK = 64
PLE_DIM = 256
MIX_WIDTH = D_MODEL
SSM_WIDTH = MIX_WIDTH // 2
SSM_GROUP = 16
SSM_GROUPS = SSM_WIDTH // SSM_GROUP
SSM_STATE = 64
SB_WIDTH = MIX_WIDTH - SSM_WIDTH
SB_HEAD_DIM = 128
SB_HEADS = SB_WIDTH // SB_HEAD_DIM
Q_BLOCK = 128
D_FF = ((8 * D_MODEL // 3 + 255) // 256) * 256
EPS = 1e-6
DT_MIN = 1e-3
DT_MAX = 1e-1
LAMBDA_RE_MAX = -1e-4

kernel_name = "hybrid_s5_stickbreaking_macaron_layer"


def rms_norm(x, gain):
    xf = x.astype(jnp.float32)
    y = xf * lax.rsqrt(jnp.mean(xf * xf, axis=-1, keepdims=True) + EPS)
    return (y * gain.astype(jnp.float32)).astype(x.dtype)


def swiglu(h, w_gate, w_up, w_down):
    return (jax.nn.silu(h @ w_gate) * (h @ w_up)) @ w_down


def _ssm_combine(left, right):
    a_l, b_l = left
    a_r, b_r = right
    return a_l * a_r, a_r * b_l + b_r


def s5_mixer(u, lam_re, lam_im, b_re, b_im, c_re, c_im, log_dt, d_skip, w_glu, b_glu):
    bsz, seq, _ = u.shape
    f32 = jnp.float32
    lam = lax.complex(jnp.minimum(lam_re.astype(f32), LAMBDA_RE_MAX), lam_im.astype(f32))
    dt = jnp.exp(log_dt.astype(f32))[:, None]
    lam_bar = jnp.exp(lam * dt)
    b_bar = ((lam_bar - 1.0) / lam)[:, :, None] * lax.complex(b_re.astype(f32), b_im.astype(f32))
    c = lax.complex(c_re.astype(f32), c_im.astype(f32))
    d = d_skip.astype(f32).reshape(SSM_GROUPS, SSM_GROUP)
    n_chunks = seq // CHUNK
    u_c = u.astype(f32).reshape(bsz, n_chunks, CHUNK, SSM_GROUPS, SSM_GROUP).transpose(1, 0, 2, 3, 4)
    a_chunk = jnp.broadcast_to(lam_bar, (bsz, CHUNK, SSM_GROUPS, SSM_STATE))

    def step(state, u_k):
        bu = jnp.einsum("bcgh,gph->bcgp", u_k, b_bar)
        a_cum, s_in = lax.associative_scan(_ssm_combine, (a_chunk, bu), axis=1)
        s = s_in + a_cum * state[:, None]
        y = jnp.real(jnp.einsum("bcgp,ghp->bcgh", s, c)) + d * u_k
        return s[:, -1], y

    state0 = jnp.zeros((bsz, SSM_GROUPS, SSM_STATE), jnp.complex64)
    _, y = lax.scan(step, state0, u_c)
    y = y.transpose(1, 0, 2, 3, 4).reshape(bsz, seq, SSM_WIDTH)
    z = jax.nn.gelu(y)
    out = z * jax.nn.sigmoid(z @ w_glu.astype(f32) + b_glu.astype(f32))
    return out.astype(u.dtype)


def stick_breaking_attention(q, k, v):
    f32 = jnp.float32
    seq = q.shape[2]
    scale = SB_HEAD_DIM ** -0.5
    qf, kf, vf = q.astype(f32), k.astype(f32), v.astype(f32)
    outs = []
    for blk in range(seq // Q_BLOCK):
        q0 = blk * Q_BLOCK
        k_end = q0 + Q_BLOCK
        z = jnp.einsum("bhqd,bhkd->bhqk", qf[:, :, q0:k_end], kf[:, :, :k_end]) * scale
        q_pos = q0 + jnp.arange(Q_BLOCK)[:, None]
        k_pos = jnp.arange(k_end)[None, :]
        before = k_pos < q_pos
        log_keep = jnp.where(before, jax.nn.log_sigmoid(-z), 0.0)
        log_pass = lax.cumsum(log_keep, axis=3, reverse=True) - log_keep
        w = jnp.where(before, jnp.exp(jax.nn.log_sigmoid(z) + log_pass), 0.0)
        outs.append(jnp.einsum("bhqk,bhkd->bhqd", w, vf[:, :, :k_end]))
    return jnp.concatenate(outs, axis=2).astype(q.dtype)


def setup_inputs(seed: int = 0) -> dict:
    key = jax.random.key(seed)
    ks = iter(jax.random.split(key, 40))
    D, F = D_MODEL, D_FF
    G, H, P = SSM_GROUPS, SSM_GROUP, SSM_STATE

    def nrm(shape, scale):
        return scale * jax.random.normal(next(ks), shape, jnp.float32)

    def gain(n):
        return 1.0 + nrm((DEPTH, n), 0.02)

    n_idx = jnp.arange(P, dtype=jnp.float32)
    return {
        "x": nrm((BATCH, SEQ, D), 1.0),
        "p": nrm((DEPTH, BATCH, SEQ, PLE_DIM), 1.0),
        "ffn1_norm": gain(D),
        "ffn1_w_gate": nrm((DEPTH, D, F), D ** -0.5),
        "ffn1_w_up": nrm((DEPTH, D, F), D ** -0.5),
        "ffn1_w_down": nrm((DEPTH, F, D), F ** -0.5),
        "mix_norm": gain(D),
        "w_in": nrm((DEPTH, D, SSM_WIDTH + 3 * SB_WIDTH), D ** -0.5),
        "ssm_lambda_re": -0.5 + nrm((DEPTH, G, P), 0.01),
        "ssm_lambda_im": math.pi * jnp.broadcast_to(n_idx, (DEPTH, G, P)) + nrm((DEPTH, G, P), 0.01),
        "ssm_b_re": nrm((DEPTH, G, P, H), (2 * H) ** -0.5),
        "ssm_b_im": nrm((DEPTH, G, P, H), (2 * H) ** -0.5),
        "ssm_c_re": nrm((DEPTH, G, H, P), (2 * P) ** -0.5),
        "ssm_c_im": nrm((DEPTH, G, H, P), (2 * P) ** -0.5),
        "ssm_log_dt": jax.random.uniform(next(ks), (DEPTH, G), jnp.float32, math.log(DT_MIN), math.log(DT_MAX)),
        "ssm_d": nrm((DEPTH, SSM_WIDTH), 1.0),
        "ssm_w_glu": nrm((DEPTH, SSM_WIDTH, SSM_WIDTH), SSM_WIDTH ** -0.5),
        "ssm_b_glu": nrm((DEPTH, SSM_WIDTH), 0.01),
        "q_norm": gain(SB_HEAD_DIM),
        "k_norm": gain(SB_HEAD_DIM),
        "out_norm_ssm": gain(SSM_WIDTH),
        "out_norm_sb": gain(SB_WIDTH),
        "w_out": nrm((DEPTH, MIX_WIDTH, D), MIX_WIDTH ** -0.5),
        "ffn2_norm": gain(D),
        "ffn2_w_gate": nrm((DEPTH, D, F), D ** -0.5),
        "ffn2_w_up": nrm((DEPTH, D, F), D ** -0.5),
        "ffn2_w_down": nrm((DEPTH, F, D), F ** -0.5),
        "ple_norm": gain(D),
        "w_ple_gate": nrm((DEPTH, D, D), D ** -0.5),
        "w_ple_proj": nrm((DEPTH, PLE_DIM, D), PLE_DIM ** -0.5),
        "ple_post_norm": gain(D),
    }


def reference(x, p, ffn1_norm, ffn1_w_gate, ffn1_w_up, ffn1_w_down, mix_norm, w_in,
              ssm_lambda_re, ssm_lambda_im, ssm_b_re, ssm_b_im, ssm_c_re, ssm_c_im,
              ssm_log_dt, ssm_d, ssm_w_glu, ssm_b_glu, q_norm, k_norm,
              out_norm_ssm, out_norm_sb, w_out, ffn2_norm, ffn2_w_gate, ffn2_w_up,
              ffn2_w_down, ple_norm, w_ple_gate, w_ple_proj, ple_post_norm):
    bsz, seq, _ = x.shape
    for i in range(DEPTH):
        x = x + 0.5 * swiglu(rms_norm(x, ffn1_norm[i]), ffn1_w_gate[i], ffn1_w_up[i], ffn1_w_down[i])

        h = rms_norm(x, mix_norm[i])
        proj = h @ w_in[i]
        u = proj[..., :SSM_WIDTH]
        q = proj[..., SSM_WIDTH:SSM_WIDTH + SB_WIDTH].reshape(bsz, seq, SB_HEADS, SB_HEAD_DIM)
        k = proj[..., SSM_WIDTH + SB_WIDTH:SSM_WIDTH + 2 * SB_WIDTH].reshape(bsz, seq, SB_HEADS, SB_HEAD_DIM)
        v = proj[..., SSM_WIDTH + 2 * SB_WIDTH:].reshape(bsz, seq, SB_HEADS, SB_HEAD_DIM)

        y_ssm = s5_mixer(u, ssm_lambda_re[i], ssm_lambda_im[i], ssm_b_re[i], ssm_b_im[i],
                         ssm_c_re[i], ssm_c_im[i], ssm_log_dt[i], ssm_d[i], ssm_w_glu[i], ssm_b_glu[i])

        q = rms_norm(q, q_norm[i]).transpose(0, 2, 1, 3)
        k = rms_norm(k, k_norm[i]).transpose(0, 2, 1, 3)
        y_sb = stick_breaking_attention(q, k, v.transpose(0, 2, 1, 3))
        y_sb = y_sb.transpose(0, 2, 1, 3).reshape(bsz, seq, SB_WIDTH)

        mixed = jnp.concatenate([rms_norm(y_ssm, out_norm_ssm[i]), rms_norm(y_sb, out_norm_sb[i])], axis=-1)
        x = x + mixed @ w_out[i]

        x = x + 0.5 * swiglu(rms_norm(x, ffn2_norm[i]), ffn2_w_gate[i], ffn2_w_up[i], ffn2_w_down[i])

        gate = jax.nn.sigmoid(rms_norm(x, ple_norm[i]) @ w_ple_gate[i])
        e = (p[i] @ w_ple_proj[i]) * gate
        x = x + rms_norm(e, ple_post_norm[i])
    return x
```

```python
import functools
import math

import jax
import jax.numpy as jnp
from jax import lax
from jax.experimental import pallas as pl
from jax.experimental.pallas import tpu as pltpu

F32 = jnp.float32
BF16 = jnp.bfloat16

D_MODEL = 2048
PLE_DIM = 256
SSM_WIDTH = 1024
SSM_GROUP = 16
SSM_GROUPS = SSM_WIDTH // SSM_GROUP
SSM_STATE = 64
SB_WIDTH = 1024
SB_HEAD_DIM = 128
SB_HEADS = SB_WIDTH // SB_HEAD_DIM
D_FF = 5632
EPS = 1e-6
LAMBDA_RE_MAX = -1e-4

SSM_CHUNK = 16
SSM_ROW = SSM_CHUNK * SSM_GROUP
LANES = 128
Q_TILE = 128
K_TILE = 256
VMEM_LIMIT = 56 * 1024 * 1024


def _params(*semantics):
    return pltpu.CompilerParams(dimension_semantics=semantics, vmem_limit_bytes=VMEM_LIMIT)


def _rms(x, gain):
    return x * lax.rsqrt(jnp.mean(x * x, axis=-1, keepdims=True) + EPS) * gain


def _dot(a, b):
    return jnp.dot(a, b, preferred_element_type=F32)


def _dot_nt(a, b, **kw):
    return lax.dot_general(a, b, (((1,), (1,)), ((), ())), preferred_element_type=F32, **kw)


def _ffn_kernel(x_ref, g_ref, wg_ref, wu_ref, wd_ref, o_ref, h_ref):
    j = pl.program_id(1)

    @pl.when(j == 0)
    def _():
        h_ref[...] = _rms(x_ref[...], g_ref[...]).astype(BF16)

    h = h_ref[...]
    a = _dot(h, wg_ref[...])
    b = _dot(h, wu_ref[...])
    act = (a * jax.nn.sigmoid(a) * (0.5 * b)).astype(BF16)
    c = _dot(act, wd_ref[...])

    @pl.when(j == 0)
    def _():
        o_ref[...] = x_ref[...] + c

    @pl.when(j > 0)
    def _():
        o_ref[...] += c


def _ffn(x, gain, w_gate, w_up, w_down, *, tm=512, tf=512):
    t, d = x.shape
    f = w_gate.shape[1]
    return pl.pallas_call(
        _ffn_kernel,
        grid=(t // tm, f // tf),
        in_specs=[
            pl.BlockSpec((tm, d), lambda i, j: (i, 0)),
            pl.BlockSpec((1, d), lambda i, j: (0, 0)),
            pl.BlockSpec((d, tf), lambda i, j: (0, j)),
            pl.BlockSpec((d, tf), lambda i, j: (0, j)),
            pl.BlockSpec((tf, d), lambda i, j: (j, 0)),
        ],
        out_specs=pl.BlockSpec((tm, d), lambda i, j: (i, 0)),
        out_shape=jax.ShapeDtypeStruct((t, d), F32),
        scratch_shapes=[pltpu.VMEM((tm, d), BF16)],
        compiler_params=_params("parallel", "arbitrary"),
        name="ffn",
    )(x, gain.reshape(1, d), w_gate, w_up, w_down)


def _in_proj_kernel(x_ref, g_ref, w_ref, qg_ref, kg_ref, o_ref, h_ref):
    j = pl.program_id(1)

    @pl.when(j == 0)
    def _():
        h_ref[...] = _rms(x_ref[...], g_ref[...]).astype(BF16)

    y = _dot(h_ref[...], w_ref[...])

    def head_norm(gain, scale):
        for hd in range(SB_HEADS):
            sl = slice(hd * SB_HEAD_DIM, (hd + 1) * SB_HEAD_DIM)
            o_ref[:, sl] = (_rms(y[:, sl], gain) * scale).astype(BF16)

    @pl.when(j == 1)
    def _():
        head_norm(qg_ref[...], SB_HEAD_DIM ** -0.5)

    @pl.when(j == 2)
    def _():
        head_norm(kg_ref[...], 1.0)

    @pl.when((j == 0) | (j == 3))
    def _():
        o_ref[...] = y.astype(BF16)


def _in_proj(x, gain, w_in, q_gain, k_gain, *, tm=512):
    t, d = x.shape
    n = w_in.shape[1]
    tn = SB_WIDTH
    return pl.pallas_call(
        _in_proj_kernel,
        grid=(t // tm, n // tn),
        in_specs=[
            pl.BlockSpec((tm, d), lambda i, j: (i, 0)),
            pl.BlockSpec((1, d), lambda i, j: (0, 0)),
            pl.BlockSpec((d, tn), lambda i, j: (0, j)),
            pl.BlockSpec((1, SB_HEAD_DIM), lambda i, j: (0, 0)),
            pl.BlockSpec((1, SB_HEAD_DIM), lambda i, j: (0, 0)),
        ],
        out_specs=pl.BlockSpec((tm, tn), lambda i, j: (i, j)),
        out_shape=jax.ShapeDtypeStruct((t, n), BF16),
        scratch_shapes=[pltpu.VMEM((tm, d), BF16)],
        compiler_params=_params("parallel", "arbitrary"),
        name="in_proj",
    )(x, gain.reshape(1, d), w_in, q_gain.reshape(1, -1), k_gain.reshape(1, -1))


def _ssm_prep_kernel(lre_ref, lim_ref, ldt_ref, bre_ref, bim_ref, cre_ref, cim_ref, d_ref,
                     k_ref, wsr_ref, wsi_ref, cor_ref, coi_ref, zr_ref, zi_ref):
    cdim, hdim, pdim = SSM_CHUNK, SSM_GROUP, SSM_STATE
    lre = jnp.minimum(lre_ref[...], LAMBDA_RE_MAX)
    lim = lim_ref[...]
    dt = jnp.exp(ldt_ref[...])
    a = lre * dt
    th = lim * dt

    def zpow(steps):
        mag = jnp.exp(steps * a)
        return mag * jnp.cos(steps * th), mag * jnp.sin(steps * th)

    z1r, z1i = zpow(jnp.ones((1, 1), F32))
    den = lre * lre + lim * lim
    cfr = ((z1r - 1.0) * lre + z1i * lim) / den
    cfi = (z1i * lre - (z1r - 1.0) * lim) / den
    btr, bti = bre_ref[...], bim_ref[...]
    bbr = cfr * btr - cfi * bti
    bbi = cfr * bti + cfi * btr
    cr, ci = cre_ref[...], cim_ref[...]

    steps = lax.broadcasted_iota(jnp.int32, (cdim, 1), 0).astype(F32)

    def outer(zr, zi, mr, mi):
        zr, zi = zr[:, None, :], zi[:, None, :]
        mr, mi = mr[None, :, :], mi[None, :, :]
        return ((zr * mr - zi * mi).reshape(cdim * hdim, pdim),
                (zr * mi + zi * mr).reshape(cdim * hdim, pdim))

    zr, zi = zpow(steps)
    czr, czi = outer(zr, zi, cr, ci)
    kk = (_dot_nt(czr, bbr, precision=lax.Precision.HIGHEST)
          - _dot_nt(czi, bbi, precision=lax.Precision.HIGHEST))
    row = lax.broadcasted_iota(jnp.int32, kk.shape, 0)
    col = lax.broadcasted_iota(jnp.int32, kk.shape, 1)
    k_ref[...] = kk + jnp.where(row == col, d_ref[...], 0.0)

    zr, zi = zpow((cdim - 1.0) - steps)
    wsr_ref[...], wsi_ref[...] = outer(zr, zi, bbr, bbi)

    zr, zi = zpow(steps + 1.0)
    cor, coi = outer(zr, zi, cr, ci)
    cor_ref[...] = cor
    coi_ref[...] = -coi

    zr_ref[...], zi_ref[...] = zpow(jnp.full((1, 1), float(cdim), F32))


def _ssm_prep(lam_re, lam_im, b_re, b_im, c_re, c_im, log_dt, d_skip):
    g, p, h, r = SSM_GROUPS, SSM_STATE, SSM_GROUP, SSM_ROW

    def per_group(*shape):
        return pl.BlockSpec((None,) + shape, lambda i: (i,) + (0,) * len(shape))

    outs = pl.pallas_call(
        _ssm_prep_kernel,
        grid=(g,),
        in_specs=[per_group(1, p), per_group(1, p), per_group(1, 1), per_group(h, p), per_group(h, p),
                  per_group(h, p), per_group(h, p), per_group(1, h)],
        out_specs=[per_group(r, h), per_group(r, p), per_group(r, p), per_group(r, p), per_group(r, p),
                   per_group(1, p), per_group(1, p)],
        out_shape=[jax.ShapeDtypeStruct((g, r, h), F32)] + [jax.ShapeDtypeStruct((g, r, p), F32)] * 4
                  + [jax.ShapeDtypeStruct((g, 1, p), F32)] * 2,
        compiler_params=_params("parallel"),
        name="ssm_prep",
    )(lam_re.reshape(g, 1, p), lam_im.reshape(g, 1, p), log_dt.reshape(g, 1, 1),
      b_re.transpose(0, 2, 1), b_im.transpose(0, 2, 1), c_re, c_im, d_skip.reshape(g, 1, h))
    kk, wsr, wsi, cor, coi, zr, zi = outs

    c = SSM_CHUNK
    kt = kk.reshape(g, c, h, h).transpose(0, 1, 3, 2)
    tau = jnp.arange(c)[:, None]
    tt = jnp.arange(c)[None, :]
    toep = jnp.where((tt >= tau)[None, :, :, None, None], kt[:, jnp.clip(tt - tau, 0, c - 1)], 0.0)
    toep = toep.transpose(0, 1, 3, 2, 4).reshape(g, r, r)
    pad = jnp.zeros((g, r, LANES - p), F32)
    w_in = jnp.concatenate([toep, wsr, pad, wsi, pad], axis=2).astype(BF16)
    padt = jnp.zeros((g, LANES - p, r), F32)
    w_out = jnp.concatenate([cor.transpose(0, 2, 1), padt, coi.transpose(0, 2, 1), padt], axis=1).astype(BF16)
    padz = jnp.zeros((g, 1, LANES - p), F32)
    return w_in, w_out, jnp.concatenate([zr, padz], axis=2), jnp.concatenate([zi, padz], axis=2)


def _ssm_scan_kernel(u_ref, win_ref, wout_ref, zr_ref, zi_ref, y_ref, ye_ref, sp_ref, *, n_batch):
    r = SSM_ROW
    ye_ref[...] = _dot(u_ref[...], win_ref[...])
    zr = jnp.broadcast_to(zr_ref[...], (n_batch, LANES))
    zi = jnp.broadcast_to(zi_ref[...], (n_batch, LANES))
    n_chunks = u_ref.shape[0] // n_batch

    def step(c, carry):
        sr, si = carry
        rows = pl.ds(pl.multiple_of(c * n_batch, n_batch), n_batch)
        sp_ref[rows, :LANES] = sr
        sp_ref[rows, LANES:] = si
        er = ye_ref[rows, r:r + LANES]
        ei = ye_ref[rows, r + LANES:]
        return zr * sr - zi * si + er, zr * si + zi * sr + ei

    zero = jnp.zeros((n_batch, LANES), F32)
    lax.fori_loop(0, n_chunks, step, (zero, zero), unroll=8)
    y_ref[...] = ye_ref[:, :r] + _dot(sp_ref[...].astype(BF16), wout_ref[...])


def _ssm_scan(u_rows, w_in, w_out, zr, zi, *, n_batch):
    g, rows, r = u_rows.shape
    per_group = lambda *shape: pl.BlockSpec((None,) + shape, lambda i: (i, 0, 0))
    return pl.pallas_call(
        functools.partial(_ssm_scan_kernel, n_batch=n_batch),
        grid=(g,),
        in_specs=[per_group(rows, r), per_group(r, r + 2 * LANES), per_group(2 * LANES, r),
                  per_group(1, LANES), per_group(1, LANES)],
        out_specs=per_group(rows, r),
        out_shape=jax.ShapeDtypeStruct((g, rows, r), F32),
        scratch_shapes=[pltpu.VMEM((rows, r + 2 * LANES), F32), pltpu.VMEM((rows, 2 * LANES), F32)],
        compiler_params=_params("parallel"),
        name="ssm_scan",
    )(u_rows, w_in, w_out, zr, zi)


def _ssm_glu_kernel(y_ref, w_ref, b_ref, g_ref, o_ref):
    y = y_ref[...]
    z = 0.5 * y * (1.0 + jnp.tanh(math.sqrt(2.0 / math.pi) * (y + 0.044715 * (y * y * y))))
    out = z * jax.nn.sigmoid(_dot(z.astype(BF16), w_ref[...]) + b_ref[...])
    o_ref[...] = _rms(out, g_ref[...]).astype(BF16)


def _ssm_glu(y, w_glu, b_glu, gain, *, tm=1024):
    t, w = y.shape
    return pl.pallas_call(
        _ssm_glu_kernel,
        grid=(t // tm,),
        in_specs=[pl.BlockSpec((tm, w), lambda i: (i, 0)), pl.BlockSpec((w, w), lambda i: (0, 0)),
                  pl.BlockSpec((1, w), lambda i: (0, 0)), pl.BlockSpec((1, w), lambda i: (0, 0))],
        out_specs=pl.BlockSpec((tm, w), lambda i: (i, 0)),
        out_shape=jax.ShapeDtypeStruct((t, w), BF16),
        compiler_params=_params("parallel"),
        name="ssm_glu",
    )(y, w_glu, b_glu.reshape(1, w), gain.reshape(1, w))


def _sb_attn_kernel(q_ref, k_ref, v_ref, o_ref):
    seq = q_ref.shape[0]
    s_idx = lax.broadcasted_iota(jnp.int32, (K_TILE, K_TILE), 0)
    j_idx = lax.broadcasted_iota(jnp.int32, (K_TILE, K_TILE), 1)
    later = (s_idx > j_idx).astype(BF16)

    def block(q, kb, run, acc, valid):
        keys = pl.ds(pl.multiple_of(kb * K_TILE, K_TILE), K_TILE)
        z = _dot_nt(q, k_ref[keys, :])
        softplus = jnp.maximum(z, 0.0) + jnp.log(1.0 + jnp.exp(-jnp.abs(z)))
        log_keep = -softplus if valid is None else jnp.where(valid, -softplus, 0.0)
        hi = log_keep.astype(BF16)
        lo = (log_keep - hi.astype(F32)).astype(BF16)
        log_pass = run + _dot(hi, later) + _dot(lo, later)
        w = jnp.exp(z - softplus + log_pass)
        if valid is not None:
            w = jnp.where(valid, w, 0.0)
        acc = acc + _dot(w.astype(BF16), v_ref[keys, :])
        return run + jnp.sum(log_keep, axis=-1, keepdims=True), acc

    def q_block(qi, _):
        q0 = pl.multiple_of(qi * Q_TILE, Q_TILE)
        q = q_ref[pl.ds(q0, Q_TILE), :]
        diag = (qi * Q_TILE) // K_TILE
        q_pos = q0 + lax.broadcasted_iota(jnp.int32, (Q_TILE, K_TILE), 0)
        k_pos = diag * K_TILE + lax.broadcasted_iota(jnp.int32, (Q_TILE, K_TILE), 1)
        run = jnp.zeros((Q_TILE, 1), F32)
        acc = jnp.zeros((Q_TILE, SB_HEAD_DIM), F32)
        run, acc = block(q, diag, run, acc, k_pos < q_pos)

        def below(n, carry):
            return block(q, diag - 1 - n, *carry, None)

        run, acc = lax.fori_loop(0, diag, below, (run, acc))
        o_ref[pl.ds(q0, Q_TILE), :] = acc
        return 0

    lax.fori_loop(0, seq // Q_TILE, q_block, 0)


def _sb_attn(proj, *, n_batch, seq):
    t = proj.shape[0]
    col0 = SSM_WIDTH // SB_HEAD_DIM
    blk = lambda off: pl.BlockSpec((seq, SB_HEAD_DIM), lambda b, h: (b, off + h))
    return pl.pallas_call(
        _sb_attn_kernel,
        grid=(n_batch, SB_HEADS),
        in_specs=[blk(col0), blk(col0 + SB_HEADS), blk(col0 + 2 * SB_HEADS)],
        out_specs=blk(0),
        out_shape=jax.ShapeDtypeStruct((t, SB_WIDTH), F32),
        compiler_params=_params("parallel", "parallel"),
        name="sb_attn",
    )(proj, proj, proj)


def _out_proj_kernel(x_ref, ms_ref, sb_ref, g_ref, w_ref, o_ref):
    sb = _rms(sb_ref[...], g_ref[...]).astype(BF16)
    o_ref[...] = (x_ref[...] + _dot(ms_ref[...], w_ref[:SSM_WIDTH, :]) + _dot(sb, w_ref[SSM_WIDTH:, :]))


def _out_proj(x, mixed_ssm, y_sb, sb_gain, w_out, *, tm=512):
    t, d = x.shape
    row = lambda w: pl.BlockSpec((tm, w), lambda i: (i, 0))
    return pl.pallas_call(
        _out_proj_kernel,
        grid=(t // tm,),
        in_specs=[row(d), row(SSM_WIDTH), row(SB_WIDTH), pl.BlockSpec((1, SB_WIDTH), lambda i: (0, 0)),
                  pl.BlockSpec(w_out.shape, lambda i: (0, 0))],
        out_specs=row(d),
        out_shape=jax.ShapeDtypeStruct((t, d), F32),
        compiler_params=_params("parallel"),
        name="out_proj",
    )(x, mixed_ssm, y_sb, sb_gain.reshape(1, -1), w_out)


def _ple_kernel(x_ref, p_ref, g_ref, wg_ref, wp_ref, pg_ref, o_ref):
    x = x_ref[...]
    gate = jax.nn.sigmoid(_dot(_rms(x, g_ref[...]).astype(BF16), wg_ref[...]))
    e = _dot(p_ref[...].astype(BF16), wp_ref[...]) * gate
    o_ref[...] = x + _rms(e, pg_ref[...])


def _ple(x, p, gain, w_gate, w_proj, post_gain, *, tm=512):
    t, d = x.shape
    row = lambda w: pl.BlockSpec((tm, w), lambda i: (i, 0))
    vec = pl.BlockSpec((1, d), lambda i: (0, 0))
    return pl.pallas_call(
        _ple_kernel,
        grid=(t // tm,),
        in_specs=[row(d), row(p.shape[1]), vec, pl.BlockSpec(w_gate.shape, lambda i: (0, 0)),
                  pl.BlockSpec(w_proj.shape, lambda i: (0, 0)), vec],
        out_specs=row(d),
        out_shape=jax.ShapeDtypeStruct((t, d), F32),
        compiler_params=_params("parallel"),
        name="ple",
    )(x, p, gain.reshape(1, d), w_gate, w_proj, post_gain.reshape(1, d))


def _s5_mixer(u, n_batch, seq, lam_re, lam_im, b_re, b_im, c_re, c_im, log_dt, d_skip, w_glu, b_glu, gain):
    g, h, c = SSM_GROUPS, SSM_GROUP, SSM_CHUNK
    n_chunks = seq // c
    w_in, w_out, zr, zi = _ssm_prep(lam_re, lam_im, b_re, b_im, c_re, c_im, log_dt, d_skip)
    u_rows = (u.reshape(n_batch, n_chunks, c, g, h).transpose(3, 1, 0, 2, 4)
              .reshape(g, n_chunks * n_batch, c * h))
    y_rows = _ssm_scan(u_rows, w_in, w_out, zr, zi, n_batch=n_batch)
    y = (y_rows.reshape(g, n_chunks, n_batch, c, h).transpose(2, 1, 3, 0, 4)
         .reshape(n_batch * seq, g * h))
    return _ssm_glu(y, w_glu.astype(BF16), b_glu, gain)


def kernel(x, p, ffn1_norm, ffn1_w_gate, ffn1_w_up, ffn1_w_down, mix_norm, w_in, ssm_lambda_re, ssm_lambda_im, ssm_b_re, ssm_b_im, ssm_c_re, ssm_c_im, ssm_log_dt, ssm_d, ssm_w_glu, ssm_b_glu, q_norm, k_norm, out_norm_ssm, out_norm_sb, w_out, ffn2_norm, ffn2_w_gate, ffn2_w_up, ffn2_w_down, ple_norm, w_ple_gate, w_ple_proj, ple_post_norm):
    n_batch, seq, d = x.shape
    xt = x.reshape(n_batch * seq, d)
    for i in range(p.shape[0]):
        xt = _ffn(xt, ffn1_norm[i], ffn1_w_gate[i].astype(BF16), ffn1_w_up[i].astype(BF16),
                  ffn1_w_down[i].astype(BF16))
        proj = _in_proj(xt, mix_norm[i], w_in[i].astype(BF16), q_norm[i], k_norm[i])
        mixed_ssm = _s5_mixer(proj[:, :SSM_WIDTH], n_batch, seq, ssm_lambda_re[i], ssm_lambda_im[i],
                              ssm_b_re[i], ssm_b_im[i], ssm_c_re[i], ssm_c_im[i], ssm_log_dt[i], ssm_d[i],
                              ssm_w_glu[i], ssm_b_glu[i], out_norm_ssm[i])
        y_sb = _sb_attn(proj, n_batch=n_batch, seq=seq)
        xt = _out_proj(xt, mixed_ssm, y_sb, out_norm_sb[i], w_out[i].astype(BF16))
        xt = _ffn(xt, ffn2_norm[i], ffn2_w_gate[i].astype(BF16), ffn2_w_up[i].astype(BF16),
                  ffn2_w_down[i].astype(BF16))
        xt = _ple(xt, p[i].reshape(n_batch * seq, -1), ple_norm[i], w_ple_gate[i].astype(BF16),
                  w_ple_proj[i].astype(BF16), ple_post_norm[i])
    return xt.reshape(n_batch, seq, d)
```

```python
import functools
import math

import jax
import jax.numpy as jnp
from jax import lax
from jax.experimental import pallas as pl
from jax.experimental.pallas import tpu as pltpu

F32 = jnp.float32
BF16 = jnp.bfloat16

D_MODEL = 2048
PLE_DIM = 256
SSM_WIDTH = 1024
SSM_GROUP = 16
SSM_GROUPS = SSM_WIDTH // SSM_GROUP
SSM_STATE = 64
SB_WIDTH = 1024
SB_HEAD_DIM = 128
SB_HEADS = SB_WIDTH // SB_HEAD_DIM
D_FF = 5632
EPS = 1e-6
LAMBDA_RE_MAX = -1e-4
LOG2_E = math.log2(math.e)

SSM_CHUNK = 16
SSM_ROW = SSM_CHUNK * SSM_GROUP
LANES = 128
SLAB_GROUPS = LANES // SSM_GROUP
SSM_SLABS = SSM_WIDTH // LANES
Q_TILE = 128
K_TILE = 256
Q_SPAN = 1024
K_SPAN = 512
VMEM_LIMIT = 56 * 1024 * 1024


def _params(*semantics):
    return pltpu.CompilerParams(dimension_semantics=semantics, vmem_limit_bytes=VMEM_LIMIT)


def _rms(x, gain):
    return x * lax.rsqrt(jnp.mean(x * x, axis=-1, keepdims=True) + EPS) * gain


def _dot(a, b):
    return jnp.dot(a, b, preferred_element_type=F32)


def _dot_nt(a, b, **kw):
    return lax.dot_general(a, b, (((1,), (1,)), ((), ())), preferred_element_type=F32, **kw)


def _ffn_kernel(x_ref, g_ref, wg_ref, wu_ref, wd_ref, o_ref, h_ref):
    j = pl.program_id(1)

    @pl.when(j == 0)
    def _():
        x = x_ref[...]
        h_ref[...] = _rms(x, g_ref[...]).astype(BF16)
        o_ref[...] = x

    h = h_ref[...]
    a = _dot(h, wg_ref[...])
    b = _dot(h, wu_ref[...])
    act = (a * jax.nn.sigmoid(a) * (0.5 * b)).astype(BF16)
    o_ref[...] += _dot(act, wd_ref[...])


def _ffn(x, gain, w_gate, w_up, w_down, *, tm=512, tf=512):
    t, d = x.shape
    f = w_gate.shape[1]
    return pl.pallas_call(
        _ffn_kernel,
        grid=(t // tm, f // tf),
        in_specs=[
            pl.BlockSpec((tm, d), lambda i, j: (i, 0)),
            pl.BlockSpec((1, d), lambda i, j: (0, 0)),
            pl.BlockSpec((d, tf), lambda i, j: (0, j)),
            pl.BlockSpec((d, tf), lambda i, j: (0, j)),
            pl.BlockSpec((tf, d), lambda i, j: (j, 0)),
        ],
        out_specs=pl.BlockSpec((tm, d), lambda i, j: (i, 0)),
        out_shape=jax.ShapeDtypeStruct((t, d), F32),
        scratch_shapes=[pltpu.VMEM((tm, d), BF16)],
        compiler_params=_params("parallel", "arbitrary"),
        name="ffn",
    )(x, gain.reshape(1, d), w_gate, w_up, w_down)


def _in_proj_kernel(x_ref, g_ref, w_ref, qg_ref, kg_ref, u_ref, o_ref, h_ref, y_ref):
    j = pl.program_id(1)

    @pl.when(j == 0)
    def _():
        h_ref[...] = _rms(x_ref[...], g_ref[...]).astype(BF16)

    y = _dot(h_ref[...], w_ref[...])

    def head_norm(gain, scale):
        for hd in range(SB_HEADS):
            sl = slice(hd * SB_HEAD_DIM, (hd + 1) * SB_HEAD_DIM)
            o_ref[:, sl] = (_rms(y[:, sl], gain) * scale).astype(BF16)

    @pl.when(j == 0)
    def _():
        for s in range(y_ref.shape[0]):
            lanes = slice(s * LANES, (s + 1) * LANES)
            y_ref[s] = y[:, lanes]
            for t in range(SSM_CHUNK):
                u_ref[t, :, lanes] = y_ref[s, pl.ds(t, u_ref.shape[1], stride=SSM_CHUNK), :].astype(BF16)

    @pl.when(j == 1)
    def _():
        head_norm(qg_ref[...], SB_HEAD_DIM ** -0.5 * LOG2_E)

    @pl.when(j == 2)
    def _():
        head_norm(kg_ref[...], 1.0)

    @pl.when(j == 3)
    def _():
        o_ref[...] = y.astype(BF16)


def _in_proj(x, gain, w_in, q_gain, k_gain, *, tm=512):
    t, d = x.shape
    n = w_in.shape[1]
    tn = SB_WIDTH
    return pl.pallas_call(
        _in_proj_kernel,
        grid=(t // tm, n // tn),
        in_specs=[
            pl.BlockSpec((tm, d), lambda i, j: (i, 0)),
            pl.BlockSpec((1, d), lambda i, j: (0, 0)),
            pl.BlockSpec((d, tn), lambda i, j: (0, j)),
            pl.BlockSpec((1, SB_HEAD_DIM), lambda i, j: (0, 0)),
            pl.BlockSpec((1, SB_HEAD_DIM), lambda i, j: (0, 0)),
        ],
        out_specs=[
            pl.BlockSpec((SSM_CHUNK, tm // SSM_CHUNK, tn), lambda i, j: (0, i, 0)),
            pl.BlockSpec((tm, tn), lambda i, j: (i, jnp.maximum(j - 1, 0))),
        ],
        out_shape=[jax.ShapeDtypeStruct((SSM_CHUNK, t // SSM_CHUNK, tn), BF16),
                   jax.ShapeDtypeStruct((t, n - tn), BF16)],
        scratch_shapes=[pltpu.VMEM((tm, d), BF16), pltpu.VMEM((tn // LANES, tm, LANES), F32)],
        compiler_params=_params("parallel", "arbitrary"),
        name="in_proj",
    )(x, gain.reshape(1, d), w_in, q_gain.reshape(1, -1), k_gain.reshape(1, -1))


def _ssm_prep_kernel(lre_ref, lim_ref, ldt_ref, bre_ref, bim_ref, cre_ref, cim_ref, d_ref,
                     k_ref, wsr_ref, wsi_ref, cor_ref, coi_ref, zr_ref, zi_ref):
    cdim, hdim, pdim = SSM_CHUNK, SSM_GROUP, SSM_STATE
    lre = jnp.minimum(lre_ref[...], LAMBDA_RE_MAX)
    lim = lim_ref[...]
    dt = jnp.exp(ldt_ref[...])
    a = lre * dt
    th = lim * dt

    def zpow(steps):
        mag = jnp.exp(steps * a)
        return mag * jnp.cos(steps * th), mag * jnp.sin(steps * th)

    z1r, z1i = zpow(jnp.ones((1, 1), F32))
    den = lre * lre + lim * lim
    cfr = ((z1r - 1.0) * lre + z1i * lim) / den
    cfi = (z1i * lre - (z1r - 1.0) * lim) / den
    btr, bti = bre_ref[...], bim_ref[...]
    bbr = cfr * btr - cfi * bti
    bbi = cfr * bti + cfi * btr
    cr, ci = cre_ref[...], cim_ref[...]

    steps = lax.broadcasted_iota(jnp.int32, (cdim, 1), 0).astype(F32)

    def outer(zr, zi, mr, mi):
        zr, zi = zr[:, None, :], zi[:, None, :]
        mr, mi = mr[None, :, :], mi[None, :, :]
        return ((zr * mr - zi * mi).reshape(cdim * hdim, pdim),
                (zr * mi + zi * mr).reshape(cdim * hdim, pdim))

    zr, zi = zpow(steps)
    czr, czi = outer(zr, zi, cr, ci)
    kk = (_dot_nt(czr, bbr, precision=lax.Precision.HIGHEST)
          - _dot_nt(czi, bbi, precision=lax.Precision.HIGHEST))
    row = lax.broadcasted_iota(jnp.int32, kk.shape, 0)
    col = lax.broadcasted_iota(jnp.int32, kk.shape, 1)
    k_ref[...] = kk + jnp.where(row == col, d_ref[...], 0.0)

    zr, zi = zpow((cdim - 1.0) - steps)
    wsr_ref[...], wsi_ref[...] = outer(zr, zi, bbr, bbi)

    zr, zi = zpow(steps + 1.0)
    cor, coi = outer(zr, zi, cr, ci)
    cor_ref[...] = cor
    coi_ref[...] = -coi

    zr_ref[...], zi_ref[...] = zpow(jnp.full((1, 1), float(cdim), F32))


def _ssm_prep(lam_re, lam_im, b_re, b_im, c_re, c_im, log_dt, d_skip):
    g, p, h, r = SSM_GROUPS, SSM_STATE, SSM_GROUP, SSM_ROW

    def per_group(*shape):
        return pl.BlockSpec((None,) + shape, lambda i: (i,) + (0,) * len(shape))

    outs = pl.pallas_call(
        _ssm_prep_kernel,
        grid=(g,),
        in_specs=[per_group(1, p), per_group(1, p), per_group(1, 1), per_group(h, p), per_group(h, p),
                  per_group(h, p), per_group(h, p), per_group(1, h)],
        out_specs=[per_group(r, h), per_group(r, p), per_group(r, p), per_group(r, p), per_group(r, p),
                   per_group(1, p), per_group(1, p)],
        out_shape=[jax.ShapeDtypeStruct((g, r, h), F32)] + [jax.ShapeDtypeStruct((g, r, p), F32)] * 4
                  + [jax.ShapeDtypeStruct((g, 1, p), F32)] * 2,
        compiler_params=_params("parallel"),
        name="ssm_prep",
    )(lam_re.reshape(g, 1, p), lam_im.reshape(g, 1, p), log_dt.reshape(g, 1, 1),
      b_re.transpose(0, 2, 1), b_im.transpose(0, 2, 1), c_re, c_im, d_skip.reshape(g, 1, h))
    kk, wsr, wsi, cor, coi, zr, zi = outs

    c, ns, gl = SSM_CHUNK, SSM_SLABS, SLAB_GROUPS
    same = jnp.eye(gl, dtype=bool)
    kt = kk.reshape(ns, gl, c, h, h)
    tau = jnp.arange(c)[:, None]
    tt = jnp.arange(c)[None, :]
    toep = jnp.where((tt >= tau)[None, None, :, :, None, None], kt[:, :, jnp.clip(tt - tau, 0, c - 1)], 0.0)
    toep = toep.transpose(0, 2, 1, 5, 3, 4)
    toep = jnp.where(same[None, None, :, None, None, :, None], toep[:, :, :, :, :, None, :], 0.0)
    toep = toep.reshape(ns, c * gl * h, c * gl * h)

    def to_state(ws):
        ws = ws.reshape(ns, gl, c, h, p).transpose(0, 2, 1, 3, 4)
        ws = jnp.where(same[None, None, :, None, :, None], ws[:, :, :, :, None, :], 0.0)
        return ws.reshape(ns, c * gl * h, gl * p)

    def from_state(co):
        co = co.reshape(ns, gl, c, h, p).transpose(0, 1, 4, 2, 3)
        co = jnp.where(same[None, :, None, None, :, None], co[:, :, :, :, None, :], 0.0)
        return co.reshape(ns, gl * p, c * gl * h)

    w_in = jnp.concatenate([toep, to_state(wsr), to_state(wsi)], axis=2).astype(BF16)
    w_out = jnp.concatenate([from_state(cor), from_state(coi)], axis=1).astype(BF16)
    return w_in, w_out, zr.reshape(ns, 1, gl * p), zi.reshape(ns, 1, gl * p)


def _ssm_scan_kernel(u_ref, win_ref, wout_ref, zr_ref, zi_ref, y_ref, ye_ref, sp_ref, st_ref):
    c_dim, n_batch, n_chunks, _ = u_ref.shape
    n_y = c_dim * LANES
    n_s = st_ref.shape[1] // 2

    @pl.when(pl.program_id(1) == 0)
    def _():
        st_ref[...] = jnp.zeros(st_ref.shape, F32)

    lhs = jnp.concatenate([u_ref[t].reshape(n_batch * n_chunks, LANES) for t in range(c_dim)], axis=1)
    ye = _dot(lhs, win_ref[...])
    n_v = n_s // LANES
    for k in range(2 * n_v):
        ye_ref[k] = ye[:, n_y + k * LANES:n_y + (k + 1) * LANES]
    lane_slab = lambda ref, k: ref[:, k * LANES:(k + 1) * LANES]
    zr = [jnp.broadcast_to(lane_slab(zr_ref, k), (n_batch, LANES)) for k in range(n_v)]
    zi = [jnp.broadcast_to(lane_slab(zi_ref, k), (n_batch, LANES)) for k in range(n_v)]

    def step(c, state):
        rows = pl.ds(c, n_batch, stride=n_chunks)
        new = []
        for k in range(n_v):
            sr, si = state[k], state[n_v + k]
            sp_ref[k, rows, :] = sr
            sp_ref[n_v + k, rows, :] = si
            new.append((zr[k] * sr - zi[k] * si + ye_ref[k, rows, :],
                        zr[k] * si + zi[k] * sr + ye_ref[n_v + k, rows, :]))
        return tuple(s[0] for s in new) + tuple(s[1] for s in new)

    state = lax.fori_loop(0, n_chunks, step, tuple(lane_slab(st_ref, k) for k in range(2 * n_v)), unroll=4)
    for k in range(2 * n_v):
        st_ref[:, k * LANES:(k + 1) * LANES] = state[k]
    s_prev = jnp.concatenate([sp_ref[k] for k in range(2 * n_v)], axis=1)
    y = ye[:, :n_y] + _dot(s_prev.astype(BF16), wout_ref[...])
    for t in range(c_dim):
        for b in range(n_batch):
            y_ref[b, pl.ds(t, n_chunks, stride=c_dim), :] = y[b * n_chunks:(b + 1) * n_chunks,
                                                              t * LANES:(t + 1) * LANES]


def _ssm_scan(u_t, w_in, w_out, zr, zi, *, n_batch, chunk_tile=64):
    c = SSM_CHUNK
    n_chunks = u_t.shape[1] // n_batch
    n_s = w_out.shape[1] // 2
    rows = n_batch * chunk_tile
    per_slab = lambda *shape: pl.BlockSpec((None,) + shape, lambda s, j: (s, 0, 0), pipeline_mode=pl.Buffered(1))
    return pl.pallas_call(
        _ssm_scan_kernel,
        grid=(SSM_SLABS, n_chunks // chunk_tile),
        in_specs=[pl.BlockSpec((c, n_batch, chunk_tile, LANES), lambda s, j: (0, 0, j, s)),
                  per_slab(*w_in.shape[1:]), per_slab(*w_out.shape[1:]), per_slab(1, n_s), per_slab(1, n_s)],
        out_specs=pl.BlockSpec((n_batch, chunk_tile * c, LANES), lambda s, j: (0, j, s)),
        out_shape=jax.ShapeDtypeStruct((n_batch, n_chunks * c, SSM_WIDTH), F32),
        scratch_shapes=[pltpu.VMEM((2 * n_s // LANES, rows, LANES), F32),
                        pltpu.VMEM((2 * n_s // LANES, rows, LANES), F32), pltpu.VMEM((n_batch, 2 * n_s), F32)],
        compiler_params=_params("parallel", "arbitrary"),
        name="ssm_scan",
    )(u_t.reshape(c, n_batch, n_chunks, SSM_WIDTH), w_in, w_out, zr, zi)


def _ssm_glu_kernel(y_ref, w_ref, b_ref, g_ref, o_ref):
    y = y_ref[...]
    z = 0.5 * y * (1.0 + jnp.tanh(math.sqrt(2.0 / math.pi) * (y + 0.044715 * (y * y * y))))
    out = z * jax.nn.sigmoid(_dot(z.astype(BF16), w_ref[...]) + b_ref[...])
    o_ref[...] = _rms(out, g_ref[...]).astype(BF16)


def _ssm_glu(y, w_glu, b_glu, gain, *, tm=1024):
    t, w = y.shape
    return pl.pallas_call(
        _ssm_glu_kernel,
        grid=(t // tm,),
        in_specs=[pl.BlockSpec((tm, w), lambda i: (i, 0)), pl.BlockSpec((w, w), lambda i: (0, 0)),
                  pl.BlockSpec((1, w), lambda i: (0, 0)), pl.BlockSpec((1, w), lambda i: (0, 0))],
        out_specs=pl.BlockSpec((tm, w), lambda i: (i, 0)),
        out_shape=jax.ShapeDtypeStruct((t, w), BF16),
        compiler_params=_params("parallel"),
        name="ssm_glu",
    )(y, w_glu, b_glu.reshape(1, w), gain.reshape(1, w))


def _sb_attn_kernel(q_ref, k_ref, v_ref, o_ref):
    seq = q_ref.shape[0]
    n_sub = Q_SPAN // Q_TILE
    diag_blocks = Q_SPAN // K_TILE
    span_blocks = K_SPAN // K_TILE
    s_idx = lax.broadcasted_iota(jnp.int32, (K_TILE, K_TILE), 0)
    j_idx = lax.broadcasted_iota(jnp.int32, (K_TILE, K_TILE), 1)
    later = (s_idx > j_idx).astype(BF16)
    later2 = jnp.concatenate([later, later], axis=0)
    row =lax.broadcasted_iota(jnp.int32, (Q_TILE, K_TILE), 0)
    col = lax.broadcasted_iota(jnp.int32, (Q_TILE, K_TILE), 1)

    def sweep(blocks, runs):
        runs = list(runs)
        n, depth = len(blocks), 4
        z, mid, cs = {}, {}, {}
        for step in range(n + depth - 1):
            i = step
            if i < n:
                _, q0, kb, _ = blocks[i]
                keys = pl.ds(pl.multiple_of(kb * K_TILE, K_TILE), K_TILE)
                z[i] = _dot_nt(q_ref[pl.ds(q0, Q_TILE), :], k_ref[keys, :])
            i = step - 1
            if 0 <= i < n:
                valid = blocks[i][3]
                zi = z.pop(i)
                neg_abs = pltpu.bitcast(pltpu.bitcast(zi, jnp.uint32) | jnp.uint32(0x80000000), F32)
                softplus = jnp.maximum(zi, 0.0) + jnp.log(1.0 + jnp.exp2(neg_abs)) * LOG2_E
                drop = softplus if valid is None else jnp.where(valid, softplus, 0.0)
                hi = drop.astype(BF16)
                lo = (drop - hi.astype(F32)).astype(BF16)
                mid[i] = (zi - softplus, jnp.concatenate([hi, lo], axis=1), jnp.sum(drop, axis=-1, keepdims=True))
            i = step - 2
            if 0 <= i < n:
                cs[i] = _dot(mid[i][1], later2)
            i = step - 3
            if 0 <= i < n:
                a, q0, kb, valid = blocks[i]
                keys = pl.ds(pl.multiple_of(kb * K_TILE, K_TILE), K_TILE)
                log_sig, _, total = mid.pop(i)
                w = jnp.exp2(log_sig - (runs[a] + cs.pop(i)))
                if valid is not None:
                    w = jnp.where(valid, w, 0.0)
                o_ref[pl.ds(q0, Q_TILE), :] += _dot(w.astype(BF16), v_ref[keys, :])
                runs[a] = runs[a] + total
        return tuple(runs)

    def q_span(qs, _):
        base = pl.multiple_of(qs * Q_SPAN, Q_SPAN)
        kb_diag = qs * diag_blocks
        o_ref[pl.ds(base, Q_SPAN), :] = jnp.zeros((Q_SPAN, SB_HEAD_DIM), F32)
        diag = []
        for d in range(diag_blocks):
            for a in range(n_sub):
                top = (a * Q_TILE) // K_TILE
                if d == 0:
                    diag.append((a, base + a * Q_TILE, kb_diag + top, top * K_TILE + col < a * Q_TILE + row))
                elif top - d >= 0:
                    diag.append((a, base + a * Q_TILE, kb_diag + top - d, None))
        runs = sweep(diag, [jnp.zeros((Q_TILE, 1), F32)] * n_sub)

        def below(n, runs):
            kb_hi = kb_diag - 1 - n * span_blocks
            return sweep([(a, base + a * Q_TILE, kb_hi - d, None)
                          for d in range(span_blocks) for a in range(n_sub)], runs)

        lax.fori_loop(0, qs * (Q_SPAN // K_SPAN), below, runs)
        return 0

    lax.fori_loop(0, seq // Q_SPAN, q_span, 0)


def _sb_attn(qkv, *, n_batch, seq):
    t = qkv.shape[0]
    blk = lambda off: pl.BlockSpec((seq, SB_HEAD_DIM), lambda b, h: (b, off + h))
    return pl.pallas_call(
        _sb_attn_kernel,
        grid=(n_batch, SB_HEADS),
        in_specs=[blk(0), blk(SB_HEADS), blk(2 * SB_HEADS)],
        out_specs=blk(0),
        out_shape=jax.ShapeDtypeStruct((t, SB_WIDTH), F32),
        compiler_params=_params("parallel", "parallel"),
        name="sb_attn",
    )(qkv, qkv, qkv)


def _out_proj_kernel(x_ref, ms_ref, sb_ref, g_ref, w_ref, o_ref):
    sb = _rms(sb_ref[...], g_ref[...]).astype(BF16)
    o_ref[...] = (x_ref[...] + _dot(ms_ref[...], w_ref[:SSM_WIDTH, :]) + _dot(sb, w_ref[SSM_WIDTH:, :]))


def _out_proj(x, mixed_ssm, y_sb, sb_gain, w_out, *, tm=512):
    t, d = x.shape
    row = lambda w: pl.BlockSpec((tm, w), lambda i: (i, 0))
    return pl.pallas_call(
        _out_proj_kernel,
        grid=(t // tm,),
        in_specs=[row(d), row(SSM_WIDTH), row(SB_WIDTH), pl.BlockSpec((1, SB_WIDTH), lambda i: (0, 0)),
                  pl.BlockSpec(w_out.shape, lambda i: (0, 0))],
        out_specs=row(d),
        out_shape=jax.ShapeDtypeStruct((t, d), F32),
        compiler_params=_params("parallel"),
        name="out_proj",
    )(x, mixed_ssm, y_sb, sb_gain.reshape(1, -1), w_out)


def _ple_kernel(x_ref, p_ref, g_ref, wg_ref, wp_ref, pg_ref, o_ref):
    x = x_ref[...]
    gate = jax.nn.sigmoid(_dot(_rms(x, g_ref[...]).astype(BF16), wg_ref[...]))
    e = _dot(p_ref[...].astype(BF16), wp_ref[...]) * gate
    o_ref[...] = x + _rms(e, pg_ref[...])


def _ple(x, p, gain, w_gate, w_proj, post_gain, *, tm=512):
    t, d = x.shape
    row = lambda w: pl.BlockSpec((tm, w), lambda i: (i, 0))
    vec = pl.BlockSpec((1, d), lambda i: (0, 0))
    return pl.pallas_call(
        _ple_kernel,
        grid=(t // tm,),
        in_specs=[row(d), row(p.shape[1]), vec, pl.BlockSpec(w_gate.shape, lambda i: (0, 0)),
                  pl.BlockSpec(w_proj.shape, lambda i: (0, 0)), vec],
        out_specs=row(d),
        out_shape=jax.ShapeDtypeStruct((t, d), F32),
        compiler_params=_params("parallel"),
        name="ple",
    )(x, p, gain.reshape(1, d), w_gate, w_proj, post_gain.reshape(1, d))


def _s5_mixer(u, n_batch, seq, lam_re, lam_im, b_re, b_im, c_re, c_im, log_dt, d_skip, w_glu, b_glu, gain):
    w_in, w_out, zr, zi = _ssm_prep(lam_re, lam_im, b_re, b_im, c_re, c_im, log_dt, d_skip)
    y = _ssm_scan(u, w_in, w_out, zr, zi, n_batch=n_batch)
    return _ssm_glu(y.reshape(n_batch * seq, SSM_WIDTH), w_glu.astype(BF16), b_glu, gain)


def kernel(x, p, ffn1_norm, ffn1_w_gate, ffn1_w_up, ffn1_w_down, mix_norm, w_in, ssm_lambda_re, ssm_lambda_im, ssm_b_re, ssm_b_im, ssm_c_re, ssm_c_im, ssm_log_dt, ssm_d, ssm_w_glu, ssm_b_glu, q_norm, k_norm, out_norm_ssm, out_norm_sb, w_out, ffn2_norm, ffn2_w_gate, ffn2_w_up, ffn2_w_down, ple_norm, w_ple_gate, w_ple_proj, ple_post_norm):
    n_batch, seq, d = x.shape
    xt = x.reshape(n_batch * seq, d)
    for i in range(p.shape[0]):
        xt = _ffn(xt, ffn1_norm[i], ffn1_w_gate[i].astype(BF16), ffn1_w_up[i].astype(BF16),
                  ffn1_w_down[i].astype(BF16))
        u, qkv = _in_proj(xt, mix_norm[i], w_in[i].astype(BF16), q_norm[i], k_norm[i])
        mixed_ssm = _s5_mixer(u, n_batch, seq, ssm_lambda_re[i], ssm_lambda_im[i],
                              ssm_b_re[i], ssm_b_im[i], ssm_c_re[i], ssm_c_im[i], ssm_log_dt[i], ssm_d[i],
                              ssm_w_glu[i], ssm_b_glu[i], out_norm_ssm[i])
        y_sb = _sb_attn(qkv, n_batch=n_batch, seq=seq)
        xt = _out_proj(xt, mixed_ssm, y_sb, out_norm_sb[i], w_out[i].astype(BF16))
        xt = _ffn(xt, ffn2_norm[i], ffn2_w_gate[i].astype(BF16), ffn2_w_up[i].astype(BF16),
                  ffn2_w_down[i].astype(BF16))
        xt = _ple(xt, p[i].reshape(n_batch * seq, -1), ple_norm[i], w_ple_gate[i].astype(BF16),
                  w_ple_proj[i].astype(BF16), ple_post_norm[i])
    return xt.reshape(n_batch, seq, d)
```

```python
import functools
import math

import jax
import jax.numpy as jnp
from jax import lax
from jax.experimental import pallas as pl
from jax.experimental.pallas import tpu as pltpu

F32 = jnp.float32
BF16 = jnp.bfloat16

D_MODEL = 2048
PLE_DIM = 256
SSM_WIDTH = 1024
SSM_GROUP = 16
SSM_GROUPS = SSM_WIDTH // SSM_GROUP
SSM_STATE = 64
SB_WIDTH = 1024
SB_HEAD_DIM = 128
SB_HEADS = SB_WIDTH // SB_HEAD_DIM
D_FF = 5632
EPS = 1e-6
LAMBDA_RE_MAX = -1e-4
LOG2_E = math.log2(math.e)

SSM_CHUNK = 16
SSM_ROW = SSM_CHUNK * SSM_GROUP
LANES = 128
SLAB_GROUPS = LANES // SSM_GROUP
SSM_SLABS = SSM_WIDTH // LANES
Q_TILE = 128
K_TILE = 256
Q_SPAN = 1024
K_SPAN = 512
VMEM_LIMIT = 56 * 1024 * 1024


def _params(*semantics):
    return pltpu.CompilerParams(dimension_semantics=semantics, vmem_limit_bytes=VMEM_LIMIT)


def _rms(x, gain):
    return x * lax.rsqrt(jnp.mean(x * x, axis=-1, keepdims=True) + EPS) * gain


def _dot(a, b):
    return jnp.dot(a, b, preferred_element_type=F32)


def _dot_nt(a, b, **kw):
    return lax.dot_general(a, b, (((1,), (1,)), ((), ())), preferred_element_type=F32, **kw)


def _ffn_kernel(x_ref, g_ref, wg_ref, wu_ref, wd_ref, o_ref, h_ref):
    j = pl.program_id(1)

    @pl.when(j == 0)
    def _():
        x = x_ref[...]
        h_ref[...] = _rms(x, g_ref[...]).astype(BF16)
        o_ref[...] = x

    h = h_ref[...]
    a = _dot(h, wg_ref[...])
    b = _dot(h, wu_ref[...])
    act = (a * jax.nn.sigmoid(a) * (0.5 * b)).astype(BF16)
    o_ref[...] += _dot(act, wd_ref[...])


def _ffn(x, gain, w_gate, w_up, w_down, *, tm=1024, tf=512):
    t, d = x.shape
    f = w_gate.shape[1]
    return pl.pallas_call(
        _ffn_kernel,
        grid=(t // tm, f // tf),
        in_specs=[
            pl.BlockSpec((tm, d), lambda i, j: (i, 0)),
            pl.BlockSpec((1, d), lambda i, j: (0, 0)),
            pl.BlockSpec((d, tf), lambda i, j: (0, j)),
            pl.BlockSpec((d, tf), lambda i, j: (0, j)),
            pl.BlockSpec((tf, d), lambda i, j: (j, 0)),
        ],
        out_specs=pl.BlockSpec((tm, d), lambda i, j: (i, 0)),
        out_shape=jax.ShapeDtypeStruct((t, d), F32),
        scratch_shapes=[pltpu.VMEM((tm, d), BF16)],
        compiler_params=_params("parallel", "arbitrary"),
        name="ffn",
    )(x, gain.reshape(1, d), w_gate, w_up, w_down)


def _in_proj_kernel(x_ref, g_ref, w_ref, qg_ref, kg_ref, u_ref, o_ref, y_ref):
    h = _rms(x_ref[...], g_ref[...]).astype(BF16)
    section = lambda j: _dot(h, w_ref[:, j * SB_WIDTH:(j + 1) * SB_WIDTH])

    def head_norm(y, gain, scale, col0):
        for hd in range(SB_HEADS):
            sl = slice(hd * SB_HEAD_DIM, (hd + 1) * SB_HEAD_DIM)
            o_ref[:, col0 + sl.start:col0 + sl.stop] = (_rms(y[:, sl], gain) * scale).astype(BF16)

    y = section(0)
    for s in range(y_ref.shape[0]):
        lanes = slice(s * LANES, (s + 1) * LANES)
        y_ref[s] = y[:, lanes]
        for t in range(SSM_CHUNK):
            u_ref[t, :, lanes] = y_ref[s, pl.ds(t, u_ref.shape[1], stride=SSM_CHUNK), :].astype(BF16)
    head_norm(section(1), qg_ref[...], SB_HEAD_DIM ** -0.5 * LOG2_E, 0)
    head_norm(section(2), kg_ref[...], 1.0, SB_WIDTH)
    o_ref[:, 2 * SB_WIDTH:] = section(3).astype(BF16)


def _in_proj(x, gain, w_in, q_gain, k_gain, *, tm=512):
    t, d = x.shape
    n = w_in.shape[1]
    const = lambda shape: pl.BlockSpec(shape, lambda i: (0, 0), pipeline_mode=pl.Buffered(1))
    return pl.pallas_call(
        _in_proj_kernel,
        grid=(t // tm,),
        in_specs=[pl.BlockSpec((tm, d), lambda i: (i, 0)), const((1, d)), const((d, n)),
                  const((1, SB_HEAD_DIM)), const((1, SB_HEAD_DIM))],
        out_specs=[pl.BlockSpec((SSM_CHUNK, tm // SSM_CHUNK, SSM_WIDTH), lambda i: (0, i, 0)),
                   pl.BlockSpec((tm, n - SSM_WIDTH), lambda i: (i, 0))],
        out_shape=[jax.ShapeDtypeStruct((SSM_CHUNK, t // SSM_CHUNK, SSM_WIDTH), BF16),
                   jax.ShapeDtypeStruct((t, n - SSM_WIDTH), BF16)],
        scratch_shapes=[pltpu.VMEM((SSM_WIDTH // LANES, tm, LANES), F32)],
        compiler_params=_params("parallel"),
        name="in_proj",
    )(x, gain.reshape(1, d), w_in, q_gain.reshape(1, -1), k_gain.reshape(1, -1))


def _ssm_prep_kernel(lre_ref, lim_ref, ldt_ref, bre_ref, bim_ref, cre_ref, cim_ref, d_ref,
                     k_ref, wsr_ref, wsi_ref, cor_ref, coi_ref, zr_ref, zi_ref):
    cdim, hdim, pdim = SSM_CHUNK, SSM_GROUP, SSM_STATE
    lre = jnp.minimum(lre_ref[...], LAMBDA_RE_MAX)
    lim = lim_ref[...]
    dt = jnp.exp(ldt_ref[...])
    a = lre * dt
    th = lim * dt

    def zpow(steps):
        mag = jnp.exp(steps * a)
        return mag * jnp.cos(steps * th), mag * jnp.sin(steps * th)

    z1r, z1i = zpow(jnp.ones((1, 1), F32))
    den = lre * lre + lim * lim
    cfr = ((z1r - 1.0) * lre + z1i * lim) / den
    cfi = (z1i * lre - (z1r - 1.0) * lim) / den
    btr, bti = bre_ref[...], bim_ref[...]
    bbr = cfr * btr - cfi * bti
    bbi = cfr * bti + cfi * btr
    cr, ci = cre_ref[...], cim_ref[...]

    steps = lax.broadcasted_iota(jnp.int32, (cdim, 1), 0).astype(F32)

    def outer(zr, zi, mr, mi):
        zr, zi = zr[:, None, :], zi[:, None, :]
        mr, mi = mr[None, :, :], mi[None, :, :]
        return ((zr * mr - zi * mi).reshape(cdim * hdim, pdim),
                (zr * mi + zi * mr).reshape(cdim * hdim, pdim))

    zr, zi = zpow(steps)
    czr, czi = outer(zr, zi, cr, ci)
    kk = (_dot_nt(czr, bbr, precision=lax.Precision.HIGHEST)
          - _dot_nt(czi, bbi, precision=lax.Precision.HIGHEST))
    row = lax.broadcasted_iota(jnp.int32, kk.shape, 0)
    col = lax.broadcasted_iota(jnp.int32, kk.shape, 1)
    k_ref[...] = kk + jnp.where(row == col, d_ref[...], 0.0)

    zr, zi = zpow((cdim - 1.0) - steps)
    wsr_ref[...], wsi_ref[...] = outer(zr, zi, bbr, bbi)

    zr, zi = zpow(steps + 1.0)
    cor, coi = outer(zr, zi, cr, ci)
    cor_ref[...] = cor
    coi_ref[...] = -coi

    zr_ref[...], zi_ref[...] = zpow(jnp.full((1, 1), float(cdim), F32))


def _ssm_prep(lam_re, lam_im, b_re, b_im, c_re, c_im, log_dt, d_skip):
    g, p, h, r = SSM_GROUPS, SSM_STATE, SSM_GROUP, SSM_ROW

    def per_group(*shape):
        return pl.BlockSpec((None,) + shape, lambda i: (i,) + (0,) * len(shape))

    outs = pl.pallas_call(
        _ssm_prep_kernel,
        grid=(g,),
        in_specs=[per_group(1, p), per_group(1, p), per_group(1, 1), per_group(h, p), per_group(h, p),
                  per_group(h, p), per_group(h, p), per_group(1, h)],
        out_specs=[per_group(r, h), per_group(r, p), per_group(r, p), per_group(r, p), per_group(r, p),
                   per_group(1, p), per_group(1, p)],
        out_shape=[jax.ShapeDtypeStruct((g, r, h), F32)] + [jax.ShapeDtypeStruct((g, r, p), F32)] * 4
                  + [jax.ShapeDtypeStruct((g, 1, p), F32)] * 2,
        compiler_params=_params("parallel"),
        name="ssm_prep",
    )(lam_re.reshape(g, 1, p), lam_im.reshape(g, 1, p), log_dt.reshape(g, 1, 1),
      b_re.transpose(0, 2, 1), b_im.transpose(0, 2, 1), c_re, c_im, d_skip.reshape(g, 1, h))
    kk, wsr, wsi, cor, coi, zr, zi = outs

    c, ns, gl = SSM_CHUNK, SSM_SLABS, SLAB_GROUPS
    same = jnp.eye(gl, dtype=bool)

    def block_diag(m):
        m = m.reshape((ns, gl) + m.shape[1:]).transpose(0, 2, 1, 3, 4).astype(BF16)
        m = jnp.where(same[None, None, :, None, :, None], m[:, :, :, :, None, :], 0)
        return m.reshape(ns, m.shape[1], gl * m.shape[3], gl * m.shape[5])

    lag_blocks = block_diag(kk.reshape(g, c, h, h).transpose(0, 1, 3, 2))
    to_state = lambda ws: block_diag(ws.reshape(g, c, h, p)).reshape(ns, c * gl * h, gl * p)
    w_state = jnp.concatenate([to_state(wsr), to_state(wsi)], axis=2)
    from_state = lambda co: (block_diag(co.reshape(g, c, h, p).transpose(0, 1, 3, 2))
                             .transpose(0, 2, 1, 3).reshape(ns, gl * p, c * gl * h))
    w_out = jnp.concatenate([from_state(cor), from_state(coi)], axis=1)
    return lag_blocks, w_state, w_out, zr.reshape(ns, 1, gl * p), zi.reshape(ns, 1, gl * p)


def _ssm_scan_kernel(u_ref, lag_ref, wst_ref, wout_ref, zr_ref, zi_ref, y_ref, toep_ref, ye_ref, sp_ref, st_ref):
    c_dim, n_batch, n_chunks, _ = u_ref.shape
    n_s = st_ref.shape[1] // 2

    @pl.when(pl.program_id(1) == 0)
    def _():
        st_ref[...] = jnp.zeros(st_ref.shape, F32)
        for tau in range(c_dim):
            for t in range(c_dim):
                blk = lag_ref[t - tau] if t >= tau else jnp.zeros((LANES, LANES), BF16)
                toep_ref[tau * LANES:(tau + 1) * LANES, t * LANES:(t + 1) * LANES] = blk

    lhs = jnp.concatenate([u_ref[t].reshape(n_batch * n_chunks, LANES) for t in range(c_dim)], axis=1)
    y_intra = _dot(lhs, toep_ref[...])
    ends = _dot(lhs, wst_ref[...])
    n_v = n_s // LANES
    for k in range(2 * n_v):
        ye_ref[k] = ends[:, k * LANES:(k + 1) * LANES]
    lane_slab = lambda ref, k: ref[:, k * LANES:(k + 1) * LANES]
    zr = [jnp.broadcast_to(lane_slab(zr_ref, k), (n_batch, LANES)) for k in range(n_v)]
    zi = [jnp.broadcast_to(lane_slab(zi_ref, k), (n_batch, LANES)) for k in range(n_v)]

    def step(c, state):
        rows = pl.ds(c, n_batch, stride=n_chunks)
        new = []
        for k in range(n_v):
            sr, si = state[k], state[n_v + k]
            sp_ref[k, rows, :] = sr
            sp_ref[n_v + k, rows, :] = si
            new.append((zr[k] * sr - zi[k] * si + ye_ref[k, rows, :],
                        zr[k] * si + zi[k] * sr + ye_ref[n_v + k, rows, :]))
        return tuple(s[0] for s in new) + tuple(s[1] for s in new)

    state = lax.fori_loop(0, n_chunks, step, tuple(lane_slab(st_ref, k) for k in range(2 * n_v)), unroll=4)
    for k in range(2 * n_v):
        st_ref[:, k * LANES:(k + 1) * LANES] = state[k]
    s_prev = jnp.concatenate([sp_ref[k] for k in range(2 * n_v)], axis=1)
    y = y_intra + _dot(s_prev.astype(BF16), wout_ref[...])
    for t in range(c_dim):
        for b in range(n_batch):
            y_ref[b, pl.ds(t, n_chunks, stride=c_dim), :] = y[b * n_chunks:(b + 1) * n_chunks,
                                                              t * LANES:(t + 1) * LANES]


def _ssm_scan(u_t, lag_blocks, w_state, w_out, zr, zi, *, n_batch, chunk_tile=64):
    c = SSM_CHUNK
    n_chunks = u_t.shape[1] // n_batch
    n_s = w_out.shape[1] // 2
    rows = n_batch * chunk_tile

    def per_slab(a):
        zeros = (0,) * (a.ndim - 1)
        return pl.BlockSpec((None,) + a.shape[1:], lambda s, j: (s,) + zeros, pipeline_mode=pl.Buffered(1))

    return pl.pallas_call(
        _ssm_scan_kernel,
        grid=(SSM_SLABS, n_chunks // chunk_tile),
        in_specs=[pl.BlockSpec((c, n_batch, chunk_tile, LANES), lambda s, j: (0, 0, j, s)),
                  per_slab(lag_blocks), per_slab(w_state), per_slab(w_out), per_slab(zr), per_slab(zi)],
        out_specs=pl.BlockSpec((n_batch, chunk_tile * c, LANES), lambda s, j: (0, j, s)),
        out_shape=jax.ShapeDtypeStruct((n_batch, n_chunks * c, SSM_WIDTH), F32),
        scratch_shapes=[pltpu.VMEM((c * LANES, c * LANES), BF16),
                        pltpu.VMEM((2 * n_s // LANES, rows, LANES), F32),
                        pltpu.VMEM((2 * n_s // LANES, rows, LANES), F32), pltpu.VMEM((n_batch, 2 * n_s), F32)],
        compiler_params=_params("parallel", "arbitrary"),
        name="ssm_scan",
    )(u_t.reshape(c, n_batch, n_chunks, SSM_WIDTH), lag_blocks, w_state, w_out, zr, zi)


def _ssm_glu_kernel(y_ref, w_ref, b_ref, g_ref, o_ref):
    y = y_ref[...]
    z = 0.5 * y * (1.0 + jnp.tanh(math.sqrt(2.0 / math.pi) * (y + 0.044715 * (y * y * y))))
    out = z * jax.nn.sigmoid(_dot(z.astype(BF16), w_ref[...]) + b_ref[...])
    o_ref[...] = _rms(out, g_ref[...]).astype(BF16)


def _ssm_glu(y, w_glu, b_glu, gain, *, tm=1024):
    t, w = y.shape
    return pl.pallas_call(
        _ssm_glu_kernel,
        grid=(t // tm,),
        in_specs=[pl.BlockSpec((tm, w), lambda i: (i, 0)), pl.BlockSpec((w, w), lambda i: (0, 0)),
                  pl.BlockSpec((1, w), lambda i: (0, 0)), pl.BlockSpec((1, w), lambda i: (0, 0))],
        out_specs=pl.BlockSpec((tm, w), lambda i: (i, 0)),
        out_shape=jax.ShapeDtypeStruct((t, w), BF16),
        compiler_params=_params("parallel"),
        name="ssm_glu",
    )(y, w_glu, b_glu.reshape(1, w), gain.reshape(1, w))


def _sb_attn_kernel(q_ref, k_ref, v_ref, o_ref):
    seq = q_ref.shape[0]
    n_sub = Q_SPAN // Q_TILE
    diag_blocks = Q_SPAN // K_TILE
    span_blocks = K_SPAN // K_TILE
    s_idx = lax.broadcasted_iota(jnp.int32, (K_TILE, K_TILE), 0)
    j_idx = lax.broadcasted_iota(jnp.int32, (K_TILE, K_TILE), 1)
    later = (s_idx > j_idx).astype(BF16)
    later2 = jnp.concatenate([later, later], axis=0)
    row =lax.broadcasted_iota(jnp.int32, (Q_TILE, K_TILE), 0)
    col = lax.broadcasted_iota(jnp.int32, (Q_TILE, K_TILE), 1)

    def sweep(blocks, runs):
        runs = list(runs)
        n, depth = len(blocks), 4
        z, mid, cs = {}, {}, {}
        for step in range(n + depth - 1):
            i = step
            if i < n:
                _, q0, kb, _ = blocks[i]
                keys = pl.ds(pl.multiple_of(kb * K_TILE, K_TILE), K_TILE)
                z[i] = _dot_nt(q_ref[pl.ds(q0, Q_TILE), :], k_ref[keys, :])
            i = step - 1
            if 0 <= i < n:
                valid = blocks[i][3]
                zi = z.pop(i)
                softplus = jnp.maximum(zi, 0.0) + jnp.log(1.0 + jnp.exp2(-jnp.abs(zi))) * LOG2_E
                drop = softplus if valid is None else jnp.where(valid, softplus, 0.0)
                hi = drop.astype(BF16)
                lo = (drop - hi.astype(F32)).astype(BF16)
                mid[i] = (zi - softplus, jnp.concatenate([hi, lo], axis=1), jnp.sum(drop, axis=-1, keepdims=True))
            i = step - 2
            if 0 <= i < n:
                cs[i] = _dot(mid[i][1], later2)
            i = step - 3
            if 0 <= i < n:
                a, q0, kb, valid = blocks[i]
                keys = pl.ds(pl.multiple_of(kb * K_TILE, K_TILE), K_TILE)
                log_sig, _, total = mid.pop(i)
                w = jnp.exp2(log_sig - (runs[a] + cs.pop(i)))
                if valid is not None:
                    w = jnp.where(valid, w, 0.0)
                o_ref[pl.ds(q0, Q_TILE), :] += _dot(w.astype(BF16), v_ref[keys, :])
                runs[a] = runs[a] + total
        return tuple(runs)

    def q_span(qs, _):
        base = pl.multiple_of(qs * Q_SPAN, Q_SPAN)
        kb_diag = qs * diag_blocks
        o_ref[pl.ds(base, Q_SPAN), :] = jnp.zeros((Q_SPAN, SB_HEAD_DIM), F32)
        diag = []
        for d in range(diag_blocks):
            for a in range(n_sub):
                top = (a * Q_TILE) // K_TILE
                if d == 0:
                    diag.append((a, base + a * Q_TILE, kb_diag + top, top * K_TILE + col < a * Q_TILE + row))
                elif top - d >= 0:
                    diag.append((a, base + a * Q_TILE, kb_diag + top - d, None))
        runs = sweep(diag, [jnp.zeros((Q_TILE, 1), F32)] * n_sub)

        def below(n, runs):
            kb_hi = kb_diag - 1 - n * span_blocks
            return sweep([(a, base + a * Q_TILE, kb_hi - d, None)
                          for d in range(span_blocks) for a in range(n_sub)], runs)

        lax.fori_loop(0, qs * (Q_SPAN // K_SPAN), below, runs)
        return 0

    lax.fori_loop(0, seq // Q_SPAN, q_span, 0)


def _sb_attn(qkv, *, n_batch, seq):
    t = qkv.shape[0]
    blk = lambda off: pl.BlockSpec((seq, SB_HEAD_DIM), lambda b, h: (b, off + h))
    return pl.pallas_call(
        _sb_attn_kernel,
        grid=(n_batch, SB_HEADS),
        in_specs=[blk(0), blk(SB_HEADS), blk(2 * SB_HEADS)],
        out_specs=blk(0),
        out_shape=jax.ShapeDtypeStruct((t, SB_WIDTH), F32),
        compiler_params=_params("parallel", "parallel"),
        name="sb_attn",
    )(qkv, qkv, qkv)


def _out_proj_kernel(x_ref, ms_ref, sb_ref, g_ref, w_ref, o_ref):
    sb = _rms(sb_ref[...], g_ref[...]).astype(BF16)
    o_ref[...] = (x_ref[...] + _dot(ms_ref[...], w_ref[:SSM_WIDTH, :]) + _dot(sb, w_ref[SSM_WIDTH:, :]))


def _out_proj(x, mixed_ssm, y_sb, sb_gain, w_out, *, tm=512):
    t, d = x.shape
    row = lambda w: pl.BlockSpec((tm, w), lambda i: (i, 0))
    return pl.pallas_call(
        _out_proj_kernel,
        grid=(t // tm,),
        in_specs=[row(d), row(SSM_WIDTH), row(SB_WIDTH), pl.BlockSpec((1, SB_WIDTH), lambda i: (0, 0)),
                  pl.BlockSpec(w_out.shape, lambda i: (0, 0))],
        out_specs=row(d),
        out_shape=jax.ShapeDtypeStruct((t, d), F32),
        compiler_params=_params("parallel"),
        name="out_proj",
    )(x, mixed_ssm, y_sb, sb_gain.reshape(1, -1), w_out)


def _ple_kernel(x_ref, p_ref, g_ref, wg_ref, wp_ref, pg_ref, o_ref):
    x = x_ref[...]
    gate = jax.nn.sigmoid(_dot(_rms(x, g_ref[...]).astype(BF16), wg_ref[...]))
    e = _dot(p_ref[...].astype(BF16), wp_ref[...]) * gate
    o_ref[...] = x + _rms(e, pg_ref[...])


def _ple(x, p, gain, w_gate, w_proj, post_gain, *, tm=512):
    t, d = x.shape
    row = lambda w: pl.BlockSpec((tm, w), lambda i: (i, 0))
    vec = pl.BlockSpec((1, d), lambda i: (0, 0))
    return pl.pallas_call(
        _ple_kernel,
        grid=(t // tm,),
        in_specs=[row(d), row(p.shape[1]), vec, pl.BlockSpec(w_gate.shape, lambda i: (0, 0)),
                  pl.BlockSpec(w_proj.shape, lambda i: (0, 0)), vec],
        out_specs=row(d),
        out_shape=jax.ShapeDtypeStruct((t, d), F32),
        compiler_params=_params("parallel"),
        name="ple",
    )(x, p, gain.reshape(1, d), w_gate, w_proj, post_gain.reshape(1, d))


def _s5_mixer(u, n_batch, seq, lam_re, lam_im, b_re, b_im, c_re, c_im, log_dt, d_skip, w_glu, b_glu, gain):
    mats = _ssm_prep(lam_re, lam_im, b_re, b_im, c_re, c_im, log_dt, d_skip)
    y = _ssm_scan(u, *mats, n_batch=n_batch)
    return _ssm_glu(y.reshape(n_batch * seq, SSM_WIDTH), w_glu.astype(BF16), b_glu, gain)


def kernel(x, p, ffn1_norm, ffn1_w_gate, ffn1_w_up, ffn1_w_down, mix_norm, w_in, ssm_lambda_re, ssm_lambda_im, ssm_b_re, ssm_b_im, ssm_c_re, ssm_c_im, ssm_log_dt, ssm_d, ssm_w_glu, ssm_b_glu, q_norm, k_norm, out_norm_ssm, out_norm_sb, w_out, ffn2_norm, ffn2_w_gate, ffn2_w_up, ffn2_w_down, ple_norm, w_ple_gate, w_ple_proj, ple_post_norm):
    n_batch, seq, d = x.shape
    xt = x.reshape(n_batch * seq, d)
    for i in range(p.shape[0]):
        xt = _ffn(xt, ffn1_norm[i], ffn1_w_gate[i].astype(BF16), ffn1_w_up[i].astype(BF16),
                  ffn1_w_down[i].astype(BF16))
        u, qkv = _in_proj(xt, mix_norm[i], w_in[i].astype(BF16), q_norm[i], k_norm[i])
        mixed_ssm = _s5_mixer(u, n_batch, seq, ssm_lambda_re[i], ssm_lambda_im[i],
                              ssm_b_re[i], ssm_b_im[i], ssm_c_re[i], ssm_c_im[i], ssm_log_dt[i], ssm_d[i],
                              ssm_w_glu[i], ssm_b_glu[i], out_norm_ssm[i])
        y_sb = _sb_attn(qkv, n_batch=n_batch, seq=seq)
        xt = _out_proj(xt, mixed_ssm, y_sb, out_norm_sb[i], w_out[i].astype(BF16))
        xt = _ffn(xt, ffn2_norm[i], ffn2_w_gate[i].astype(BF16), ffn2_w_up[i].astype(BF16),
                  ffn2_w_down[i].astype(BF16))
        xt = _ple(xt, p[i].reshape(n_batch * seq, -1), ple_norm[i], w_ple_gate[i].astype(BF16),
                  w_ple_proj[i].astype(BF16), ple_post_norm[i])
    return xt.reshape(n_batch, seq, d)
```

```python
import math

import jax
import jax.numpy as jnp
from jax import lax
from jax.experimental import pallas as pl
from jax.experimental.pallas import tpu as pltpu

F32 = jnp.float32
BF16 = jnp.bfloat16

D_MODEL = 2048
PLE_DIM = 256
SSM_WIDTH = 1024
SSM_GROUP = 16
SSM_GROUPS = SSM_WIDTH // SSM_GROUP
SSM_STATE = 64
SB_WIDTH = 1024
SB_HEAD_DIM = 128
SB_HEADS = SB_WIDTH // SB_HEAD_DIM
D_FF = 5632
EPS = 1e-6
LAMBDA_RE_MAX = -1e-4
LOG2_E = math.log2(math.e)

SSM_CHUNK = 16
SSM_ROW = SSM_CHUNK * SSM_GROUP
LANES = 128
SLAB_GROUPS = LANES // SSM_GROUP
SSM_SLABS = SSM_WIDTH // LANES
Q_TILE = 256
K_TILE = 256
Q_SPAN = 1024
K_SPAN = 1024
VMEM_LIMIT = 56 * 1024 * 1024


def _params(*semantics):
    return pltpu.CompilerParams(dimension_semantics=semantics, vmem_limit_bytes=VMEM_LIMIT)


def _rms(x, gain):
    return x * lax.rsqrt(jnp.mean(x * x, axis=-1, keepdims=True) + EPS) * gain


def _dot(a, b):
    return jnp.dot(a, b, preferred_element_type=F32)


def _dot_nt(a, b, **kw):
    return lax.dot_general(a, b, (((1,), (1,)), ((), ())), preferred_element_type=F32, **kw)


def _ffn_kernel(x_ref, g_ref, wg_ref, wu_ref, wd_ref, o_ref, h_ref):
    j = pl.program_id(1)

    @pl.when(j == 0)
    def _():
        x = x_ref[...]
        h_ref[...] = _rms(x, g_ref[...]).astype(BF16)
        o_ref[...] = x

    h = h_ref[...]
    a = _dot(h, wg_ref[...])
    b = _dot(h, wu_ref[...])
    act = (a * jax.nn.sigmoid(a) * (0.5 * b)).astype(BF16)
    o_ref[...] += _dot(act, wd_ref[...])


def _ffn(x, gain, w_gate, w_up, w_down, *, tm=1024, tf=512):
    t, d = x.shape
    f = w_gate.shape[1]
    return pl.pallas_call(
        _ffn_kernel,
        grid=(t // tm, f // tf),
        in_specs=[
            pl.BlockSpec((tm, d), lambda i, j: (i, 0)),
            pl.BlockSpec((1, d), lambda i, j: (0, 0)),
            pl.BlockSpec((d, tf), lambda i, j: (0, j)),
            pl.BlockSpec((d, tf), lambda i, j: (0, j)),
            pl.BlockSpec((tf, d), lambda i, j: (j, 0)),
        ],
        out_specs=pl.BlockSpec((tm, d), lambda i, j: (i, 0)),
        out_shape=jax.ShapeDtypeStruct((t, d), F32),
        scratch_shapes=[pltpu.VMEM((tm, d), BF16)],
        compiler_params=_params("parallel", "arbitrary"),
        name="ffn",
    )(x, gain.reshape(1, d), w_gate, w_up, w_down)


def _in_proj_kernel(x_ref, g_ref, w_ref, qg_ref, kg_ref, u_ref, o_ref, y_ref):
    h = _rms(x_ref[...], g_ref[...]).astype(BF16)
    section = lambda j: _dot(h, w_ref[:, j * SB_WIDTH:(j + 1) * SB_WIDTH])

    def head_norm(y, gain, scale, col0):
        for hd in range(SB_HEADS):
            sl = slice(hd * SB_HEAD_DIM, (hd + 1) * SB_HEAD_DIM)
            o_ref[:, col0 + sl.start:col0 + sl.stop] = (_rms(y[:, sl], gain) * scale).astype(BF16)

    y = section(0)
    for s in range(y_ref.shape[0]):
        lanes = slice(s * LANES, (s + 1) * LANES)
        y_ref[s] = y[:, lanes]
        for t in range(SSM_CHUNK):
            u_ref[t, :, lanes] = y_ref[s, pl.ds(t, u_ref.shape[1], stride=SSM_CHUNK), :].astype(BF16)
    head_norm(section(1), qg_ref[...], SB_HEAD_DIM ** -0.5 * LOG2_E, 0)
    head_norm(section(2), kg_ref[...], 1.0, SB_WIDTH)
    o_ref[:, 2 * SB_WIDTH:] = section(3).astype(BF16)


def _in_proj(x, gain, w_in, q_gain, k_gain, *, tm=512):
    t, d = x.shape
    n = w_in.shape[1]
    const = lambda shape: pl.BlockSpec(shape, lambda i: (0, 0), pipeline_mode=pl.Buffered(1))
    return pl.pallas_call(
        _in_proj_kernel,
        grid=(t // tm,),
        in_specs=[pl.BlockSpec((tm, d), lambda i: (i, 0)), const((1, d)), const((d, n)),
                  const((1, SB_HEAD_DIM)), const((1, SB_HEAD_DIM))],
        out_specs=[pl.BlockSpec((SSM_CHUNK, tm // SSM_CHUNK, SSM_WIDTH), lambda i: (0, i, 0)),
                   pl.BlockSpec((tm, n - SSM_WIDTH), lambda i: (i, 0))],
        out_shape=[jax.ShapeDtypeStruct((SSM_CHUNK, t // SSM_CHUNK, SSM_WIDTH), BF16),
                   jax.ShapeDtypeStruct((t, n - SSM_WIDTH), BF16)],
        scratch_shapes=[pltpu.VMEM((SSM_WIDTH // LANES, tm, LANES), F32)],
        compiler_params=_params("parallel"),
        name="in_proj",
    )(x, gain.reshape(1, d), w_in, q_gain.reshape(1, -1), k_gain.reshape(1, -1))


def _ssm_prep_kernel(lre_ref, lim_ref, ldt_ref, bre_ref, bim_ref, cre_ref, cim_ref, d_ref,
                     k_ref, wsr_ref, wsi_ref, cor_ref, coi_ref, zr_ref, zi_ref):
    cdim, hdim, pdim = SSM_CHUNK, SSM_GROUP, SSM_STATE
    lre = jnp.minimum(lre_ref[...], LAMBDA_RE_MAX)
    lim = lim_ref[...]
    dt = jnp.exp(ldt_ref[...])
    a = lre * dt
    th = lim * dt

    def zpow(steps):
        mag = jnp.exp(steps * a)
        return mag * jnp.cos(steps * th), mag * jnp.sin(steps * th)

    z1r, z1i = zpow(jnp.ones((1, 1), F32))
    den = lre * lre + lim * lim
    cfr = ((z1r - 1.0) * lre + z1i * lim) / den
    cfi = (z1i * lre - (z1r - 1.0) * lim) / den
    btr, bti = bre_ref[...], bim_ref[...]
    bbr = cfr * btr - cfi * bti
    bbi = cfr * bti + cfi * btr
    cr, ci = cre_ref[...], cim_ref[...]

    steps = lax.broadcasted_iota(jnp.int32, (cdim, 1), 0).astype(F32)

    def outer(zr, zi, mr, mi):
        zr, zi = zr[:, None, :], zi[:, None, :]
        mr, mi = mr[None, :, :], mi[None, :, :]
        return ((zr * mr - zi * mi).reshape(cdim * hdim, pdim),
                (zr * mi + zi * mr).reshape(cdim * hdim, pdim))

    zr, zi = zpow(steps)
    czr, czi = outer(zr, zi, cr, ci)
    kk = (_dot_nt(czr, bbr, precision=lax.Precision.HIGHEST)
          - _dot_nt(czi, bbi, precision=lax.Precision.HIGHEST))
    row = lax.broadcasted_iota(jnp.int32, kk.shape, 0)
    col = lax.broadcasted_iota(jnp.int32, kk.shape, 1)
    k_ref[...] = kk + jnp.where(row == col, d_ref[...], 0.0)

    zr, zi = zpow((cdim - 1.0) - steps)
    wsr_ref[...], wsi_ref[...] = outer(zr, zi, bbr, bbi)

    zr, zi = zpow(steps + 1.0)
    cor, coi = outer(zr, zi, cr, ci)
    cor_ref[...] = cor
    coi_ref[...] = -coi

    zr_ref[...], zi_ref[...] = zpow(jnp.full((1, 1), float(cdim), F32))


def _ssm_prep(lam_re, lam_im, b_re, b_im, c_re, c_im, log_dt, d_skip):
    g, p, h, r = SSM_GROUPS, SSM_STATE, SSM_GROUP, SSM_ROW

    def per_group(*shape):
        return pl.BlockSpec((None,) + shape, lambda i: (i,) + (0,) * len(shape))

    outs = pl.pallas_call(
        _ssm_prep_kernel,
        grid=(g,),
        in_specs=[per_group(1, p), per_group(1, p), per_group(1, 1), per_group(h, p), per_group(h, p),
                  per_group(h, p), per_group(h, p), per_group(1, h)],
        out_specs=[per_group(r, h), per_group(r, p), per_group(r, p), per_group(r, p), per_group(r, p),
                   per_group(1, p), per_group(1, p)],
        out_shape=[jax.ShapeDtypeStruct((g, r, h), F32)] + [jax.ShapeDtypeStruct((g, r, p), F32)] * 4
                  + [jax.ShapeDtypeStruct((g, 1, p), F32)] * 2,
        compiler_params=_params("parallel"),
        name="ssm_prep",
    )(lam_re.reshape(g, 1, p), lam_im.reshape(g, 1, p), log_dt.reshape(g, 1, 1),
      b_re.transpose(0, 2, 1), b_im.transpose(0, 2, 1), c_re, c_im, d_skip.reshape(g, 1, h))
    kk, wsr, wsi, cor, coi, zr, zi = outs

    c, ns, gl = SSM_CHUNK, SSM_SLABS, SLAB_GROUPS
    same = jnp.eye(gl, dtype=bool)
    kt = kk.reshape(ns, gl, c, h, h).transpose(0, 2, 1, 4, 3).astype(BF16)
    lag_blocks = jnp.where(same[None, None, :, None, :, None], kt[:, :, :, :, None, :], 0)
    lag_blocks = lag_blocks.reshape(ns, c, gl * h, gl * h)

    def per_step(m):
        m = m.reshape(ns, gl, c, h, p).transpose(0, 2, 1, 3, 4).reshape(ns, c, gl * h, p).astype(BF16)
        return jnp.concatenate([m, m], axis=3)

    st_blocks = jnp.stack([per_step(wsr), per_step(wsi)], axis=1)
    out_blocks = jnp.stack([per_step(cor), per_step(coi)], axis=1)
    return lag_blocks, st_blocks, out_blocks, zr.reshape(ns, 1, gl * p), zi.reshape(ns, 1, gl * p)


def _ssm_scan_kernel(u_ref, lag_ref, stb_ref, outb_ref, zr_ref, zi_ref, y_ref,
                     toep_ref, wst_ref, wout_ref, ye_ref, sp_ref, st_ref):
    c_dim, n_batch, n_chunks, _ = u_ref.shape
    n_s = st_ref.shape[1] // 2

    @pl.when(pl.program_id(1) == 0)
    def _():
        st_ref[...] = jnp.zeros(st_ref.shape, F32)
        row_group = lax.broadcasted_iota(jnp.int32, (LANES, n_s), 0) // SSM_GROUP
        col_group = lax.broadcasted_iota(jnp.int32, (LANES, n_s), 1) // SSM_STATE
        own_group = row_group == col_group
        for tau in range(c_dim):
            rows = slice(tau * LANES, (tau + 1) * LANES)
            for t in range(c_dim):
                blk = lag_ref[t - tau] if t >= tau else jnp.zeros((LANES, LANES), BF16)
                toep_ref[rows, t * LANES:(t + 1) * LANES] = blk
            for part in range(2):
                cols = slice(part * n_s, (part + 1) * n_s)
                tile = lambda blk: jnp.concatenate([blk] * (n_s // LANES), axis=1)
                wst_ref[rows, cols] = jnp.where(own_group, tile(stb_ref[part, tau]), 0).astype(BF16)
                wout_ref[rows, cols] = jnp.where(own_group, tile(outb_ref[part, tau]), 0).astype(BF16)

    lhs = jnp.concatenate([u_ref[t].reshape(n_batch * n_chunks, LANES) for t in range(c_dim)], axis=1)
    wide = 2 * LANES
    y_intra = [_dot(lhs[:, :(tb + 1) * wide], toep_ref[:(tb + 1) * wide, tb * wide:(tb + 1) * wide])
               for tb in range(c_dim * LANES // wide)]
    ends = _dot(lhs, wst_ref[...])
    n_v = n_s // LANES
    for k in range(2 * n_v):
        ye_ref[k] = ends[:, k * LANES:(k + 1) * LANES]
    lane_slab = lambda ref, k: ref[:, k * LANES:(k + 1) * LANES]
    zr = [jnp.broadcast_to(lane_slab(zr_ref, k), (n_batch, LANES)) for k in range(n_v)]
    zi = [jnp.broadcast_to(lane_slab(zi_ref, k), (n_batch, LANES)) for k in range(n_v)]

    def step(c, state):
        rows = pl.ds(c, n_batch, stride=n_chunks)
        new = []
        for k in range(n_v):
            sr, si = state[k], state[n_v + k]
            sp_ref[k, rows, :] = sr
            sp_ref[n_v + k, rows, :] = si
            new.append((zr[k] * sr - zi[k] * si + ye_ref[k, rows, :],
                        zr[k] * si + zi[k] * sr + ye_ref[n_v + k, rows, :]))
        return tuple(s[0] for s in new) + tuple(s[1] for s in new)

    state = lax.fori_loop(0, n_chunks, step, tuple(lane_slab(st_ref, k) for k in range(2 * n_v)), unroll=4)
    for k in range(2 * n_v):
        st_ref[:, k * LANES:(k + 1) * LANES] = state[k]
    s_prev = jnp.concatenate([sp_ref[k] for k in range(2 * n_v)], axis=1)
    y_carry = _dot_nt(s_prev.astype(BF16), wout_ref[...])
    for t in range(c_dim):
        lo = t * LANES % wide
        y_t = y_intra[t * LANES // wide][:, lo:lo + LANES] + y_carry[:, t * LANES:(t + 1) * LANES]
        for b in range(n_batch):
            y_ref[b, pl.ds(t, n_chunks, stride=c_dim), :] = y_t[b * n_chunks:(b + 1) * n_chunks]


def _ssm_scan(u_t, lag_blocks, st_blocks, out_blocks, zr, zi, *, n_batch, chunk_tile=64):
    c = SSM_CHUNK
    n_chunks = u_t.shape[1] // n_batch
    n_s = zr.shape[2]
    rows = n_batch * chunk_tile

    def per_slab(a):
        zeros = (0,) * (a.ndim - 1)
        return pl.BlockSpec((None,) + a.shape[1:], lambda s, j: (s,) + zeros, pipeline_mode=pl.Buffered(1))

    return pl.pallas_call(
        _ssm_scan_kernel,
        grid=(SSM_SLABS, n_chunks // chunk_tile),
        in_specs=[pl.BlockSpec((c, n_batch, chunk_tile, LANES), lambda s, j: (0, 0, j, s)),
                  per_slab(lag_blocks), per_slab(st_blocks), per_slab(out_blocks), per_slab(zr), per_slab(zi)],
        out_specs=pl.BlockSpec((n_batch, chunk_tile * c, LANES), lambda s, j: (0, j, s)),
        out_shape=jax.ShapeDtypeStruct((n_batch, n_chunks * c, SSM_WIDTH), F32),
        scratch_shapes=[pltpu.VMEM((c * LANES, c * LANES), BF16),
                        pltpu.VMEM((c * LANES, 2 * n_s), BF16), pltpu.VMEM((c * LANES, 2 * n_s), BF16),
                        pltpu.VMEM((2 * n_s // LANES, rows, LANES), F32),
                        pltpu.VMEM((2 * n_s // LANES, rows, LANES), F32), pltpu.VMEM((n_batch, 2 * n_s), F32)],
        compiler_params=_params("parallel", "arbitrary"),
        name="ssm_scan",
    )(u_t.reshape(c, n_batch, n_chunks, SSM_WIDTH), lag_blocks, st_blocks, out_blocks, zr, zi)


def _ssm_glu_kernel(y_ref, w_ref, b_ref, g_ref, o_ref):
    y = y_ref[...]
    z = 0.5 * y * (1.0 + jnp.tanh(math.sqrt(2.0 / math.pi) * (y + 0.044715 * (y * y * y))))
    out = z * jax.nn.sigmoid(_dot(z.astype(BF16), w_ref[...]) + b_ref[...])
    o_ref[...] = _rms(out, g_ref[...]).astype(BF16)


def _ssm_glu(y, w_glu, b_glu, gain, *, tm=1024):
    t, w = y.shape
    return pl.pallas_call(
        _ssm_glu_kernel,
        grid=(t // tm,),
        in_specs=[pl.BlockSpec((tm, w), lambda i: (i, 0)), pl.BlockSpec((w, w), lambda i: (0, 0)),
                  pl.BlockSpec((1, w), lambda i: (0, 0)), pl.BlockSpec((1, w), lambda i: (0, 0))],
        out_specs=pl.BlockSpec((tm, w), lambda i: (i, 0)),
        out_shape=jax.ShapeDtypeStruct((t, w), BF16),
        compiler_params=_params("parallel"),
        name="ssm_glu",
    )(y, w_glu, b_glu.reshape(1, w), gain.reshape(1, w))


def _sb_attn_kernel(q_ref, k_ref, v_ref, o_ref, run_ref):
    seq = q_ref.shape[0]
    n_sub = Q_SPAN // Q_TILE
    diag_blocks = Q_SPAN // K_TILE
    span_blocks = K_SPAN // K_TILE
    s_idx = lax.broadcasted_iota(jnp.int32, (K_TILE, K_TILE), 0)
    j_idx = lax.broadcasted_iota(jnp.int32, (K_TILE, K_TILE), 1)
    later = (s_idx > j_idx).astype(BF16)
    later2 = jnp.concatenate([later, later], axis=0)
    row =lax.broadcasted_iota(jnp.int32, (Q_TILE, K_TILE), 0)
    col = lax.broadcasted_iota(jnp.int32, (Q_TILE, K_TILE), 1)

    def sweep(base, blocks):
        n, depth = len(blocks), 4
        z, mid, cs = {}, {}, {}
        for step in range(n + depth - 1):
            i = step
            if i < n:
                r0, nr, kb, _ = blocks[i]
                keys = pl.ds(pl.multiple_of(kb * K_TILE, K_TILE), K_TILE)
                z[i] = _dot_nt(q_ref[pl.ds(base + r0, nr), :], k_ref[keys, :])
            i = step - 1
            if 0 <= i < n:
                valid = blocks[i][3]
                zi = z.pop(i)
                softplus = jnp.maximum(zi, 0.0) + jnp.log(1.0 + jnp.exp2(-jnp.abs(zi))) * LOG2_E
                drop = softplus if valid is None else jnp.where(valid, softplus, 0.0)
                hi = drop.astype(BF16)
                lo = (drop - hi.astype(F32)).astype(BF16)
                mid[i] = (zi - softplus, jnp.concatenate([hi, lo], axis=1), jnp.sum(drop, axis=-1, keepdims=True))
            i = step - 2
            if 0 <= i < n:
                cs[i] = _dot(mid[i][1], later2)
            i = step - 3
            if 0 <= i < n:
                r0, nr, kb, valid = blocks[i]
                keys = pl.ds(pl.multiple_of(kb * K_TILE, K_TILE), K_TILE)
                log_sig, _, total = mid.pop(i)
                run = run_ref[r0:r0 + nr, :]
                w = jnp.exp2(log_sig - (run + cs.pop(i)))
                if valid is not None:
                    w = jnp.where(valid, w, 0.0)
                o_ref[pl.ds(base + r0, nr), :] += _dot(w.astype(BF16), v_ref[keys, :])
                run_ref[r0:r0 + nr, :] = run + total

    def q_span(qs, _):
        base = pl.multiple_of(qs * Q_SPAN, Q_SPAN)
        kb_diag = qs * diag_blocks
        o_ref[pl.ds(base, Q_SPAN), :] = jnp.zeros((Q_SPAN, SB_HEAD_DIM), F32)
        run_ref[...] = jnp.zeros(run_ref.shape, F32)
        diag = []
        for d in range(diag_blocks):
            for a in range(n_sub):
                top = (a * Q_TILE) // K_TILE
                if d == 0:
                    diag.append((a * Q_TILE, Q_TILE, kb_diag + top, top * K_TILE + col < a * Q_TILE + row))
                elif top - d >= 0:
                    diag.append((a * Q_TILE, Q_TILE, kb_diag + top - d, None))
        sweep(base, diag)

        def below(n, _):
            kb_hi = kb_diag - 1 - n * span_blocks
            sweep(base, [(a * Q_TILE, Q_TILE, kb_hi - d, None) for d in range(span_blocks) for a in range(n_sub)])
            return 0

        lax.fori_loop(0, qs * (Q_SPAN // K_SPAN), below, 0)
        return 0

    lax.fori_loop(0, seq // Q_SPAN, q_span, 0)


def _sb_attn(qkv, *, n_batch, seq):
    t = qkv.shape[0]
    blk = lambda off: pl.BlockSpec((seq, SB_HEAD_DIM), lambda b, h: (b, off + h))
    return pl.pallas_call(
        _sb_attn_kernel,
        grid=(n_batch, SB_HEADS),
        in_specs=[blk(0), blk(SB_HEADS), blk(2 * SB_HEADS)],
        out_specs=blk(0),
        out_shape=jax.ShapeDtypeStruct((t, SB_WIDTH), F32),
        scratch_shapes=[pltpu.VMEM((Q_SPAN, 1), F32)],
        compiler_params=_params("parallel", "parallel"),
        name="sb_attn",
    )(qkv, qkv, qkv)


def _out_proj_kernel(x_ref, ms_ref, sb_ref, g_ref, w_ref, o_ref):
    sb = _rms(sb_ref[...], g_ref[...]).astype(BF16)
    o_ref[...] = (x_ref[...] + _dot(ms_ref[...], w_ref[:SSM_WIDTH, :]) + _dot(sb, w_ref[SSM_WIDTH:, :]))


def _out_proj(x, mixed_ssm, y_sb, sb_gain, w_out, *, tm=512):
    t, d = x.shape
    row = lambda w: pl.BlockSpec((tm, w), lambda i: (i, 0))
    return pl.pallas_call(
        _out_proj_kernel,
        grid=(t // tm,),
        in_specs=[row(d), row(SSM_WIDTH), row(SB_WIDTH), pl.BlockSpec((1, SB_WIDTH), lambda i: (0, 0)),
                  pl.BlockSpec(w_out.shape, lambda i: (0, 0))],
        out_specs=row(d),
        out_shape=jax.ShapeDtypeStruct((t, d), F32),
        compiler_params=_params("parallel"),
        name="out_proj",
    )(x, mixed_ssm, y_sb, sb_gain.reshape(1, -1), w_out)


def _ple_kernel(x_ref, p_ref, g_ref, wg_ref, wp_ref, pg_ref, o_ref):
    x = x_ref[...]
    gate = jax.nn.sigmoid(_dot(_rms(x, g_ref[...]).astype(BF16), wg_ref[...]))
    e = _dot(p_ref[...].astype(BF16), wp_ref[...]) * gate
    o_ref[...] = x + _rms(e, pg_ref[...])


def _ple(x, p, gain, w_gate, w_proj, post_gain, *, tm=512):
    t, d = x.shape
    row = lambda w: pl.BlockSpec((tm, w), lambda i: (i, 0))
    vec = pl.BlockSpec((1, d), lambda i: (0, 0))
    return pl.pallas_call(
        _ple_kernel,
        grid=(t // tm,),
        in_specs=[row(d), row(p.shape[1]), vec, pl.BlockSpec(w_gate.shape, lambda i: (0, 0)),
                  pl.BlockSpec(w_proj.shape, lambda i: (0, 0)), vec],
        out_specs=row(d),
        out_shape=jax.ShapeDtypeStruct((t, d), F32),
        compiler_params=_params("parallel"),
        name="ple",
    )(x, p, gain.reshape(1, d), w_gate, w_proj, post_gain.reshape(1, d))


def _s5_mixer(u, n_batch, seq, lam_re, lam_im, b_re, b_im, c_re, c_im, log_dt, d_skip, w_glu, b_glu, gain):
    mats = _ssm_prep(lam_re, lam_im, b_re, b_im, c_re, c_im, log_dt, d_skip)
    y = _ssm_scan(u, *mats, n_batch=n_batch)
    return _ssm_glu(y.reshape(n_batch * seq, SSM_WIDTH), w_glu.astype(BF16), b_glu, gain)


def kernel(x, p, ffn1_norm, ffn1_w_gate, ffn1_w_up, ffn1_w_down, mix_norm, w_in, ssm_lambda_re, ssm_lambda_im, ssm_b_re, ssm_b_im, ssm_c_re, ssm_c_im, ssm_log_dt, ssm_d, ssm_w_glu, ssm_b_glu, q_norm, k_norm, out_norm_ssm, out_norm_sb, w_out, ffn2_norm, ffn2_w_gate, ffn2_w_up, ffn2_w_down, ple_norm, w_ple_gate, w_ple_proj, ple_post_norm):
    n_batch, seq, d = x.shape
    xt = x.reshape(n_batch * seq, d)
    for i in range(p.shape[0]):
        xt = _ffn(xt, ffn1_norm[i], ffn1_w_gate[i].astype(BF16), ffn1_w_up[i].astype(BF16),
                  ffn1_w_down[i].astype(BF16))
        u, qkv = _in_proj(xt, mix_norm[i], w_in[i].astype(BF16), q_norm[i], k_norm[i])
        mixed_ssm = _s5_mixer(u, n_batch, seq, ssm_lambda_re[i], ssm_lambda_im[i],
                              ssm_b_re[i], ssm_b_im[i], ssm_c_re[i], ssm_c_im[i], ssm_log_dt[i], ssm_d[i],
                              ssm_w_glu[i], ssm_b_glu[i], out_norm_ssm[i])
        y_sb = _sb_attn(qkv, n_batch=n_batch, seq=seq)
        xt = _out_proj(xt, mixed_ssm, y_sb, out_norm_sb[i], w_out[i].astype(BF16))
        xt = _ffn(xt, ffn2_norm[i], ffn2_w_gate[i].astype(BF16), ffn2_w_up[i].astype(BF16),
                  ffn2_w_down[i].astype(BF16))
        xt = _ple(xt, p[i].reshape(n_batch * seq, -1), ple_norm[i], w_ple_gate[i].astype(BF16),
                  w_ple_proj[i].astype(BF16), ple_post_norm[i])
    return xt.reshape(n_batch, seq, d)
```

```python
import math

import jax
import jax.numpy as jnp
from jax import lax
from jax.experimental import pallas as pl
from jax.experimental.pallas import tpu as pltpu

F32 = jnp.float32
BF16 = jnp.bfloat16

D_MODEL = 2048
PLE_DIM = 256
SSM_WIDTH = 1024
SSM_GROUP = 16
SSM_GROUPS = SSM_WIDTH // SSM_GROUP
SSM_STATE = 64
SB_WIDTH = 1024
SB_HEAD_DIM = 128
SB_HEADS = SB_WIDTH // SB_HEAD_DIM
D_FF = 5632
EPS = 1e-6
LAMBDA_RE_MAX = -1e-4
LOG2_E = math.log2(math.e)

SSM_CHUNK = 16
SSM_ROW = SSM_CHUNK * SSM_GROUP
LANES = 128
SLAB_GROUPS = LANES // SSM_GROUP
SSM_SLABS = SSM_WIDTH // LANES
Q_TILE = 256
K_TILE = 256
Q_SPAN = 1024
K_SPAN = 1024
VMEM_LIMIT = 56 * 1024 * 1024


def _params(*semantics):
    return pltpu.CompilerParams(dimension_semantics=semantics, vmem_limit_bytes=VMEM_LIMIT)


def _rms(x, gain):
    return x * lax.rsqrt(jnp.mean(x * x, axis=-1, keepdims=True) + EPS) * gain


def _dot(a, b):
    return jnp.dot(a, b, preferred_element_type=F32)


def _dot_nt(a, b, **kw):
    return lax.dot_general(a, b, (((1,), (1,)), ((), ())), preferred_element_type=F32, **kw)


def _ffn_kernel(x_ref, g_ref, wg_ref, wu_ref, wd_ref, o_ref, h_ref):
    j = pl.program_id(1)

    @pl.when(j == 0)
    def _():
        x = x_ref[...]
        h_ref[...] = _rms(x, g_ref[...]).astype(BF16)
        o_ref[...] = x

    h = h_ref[...]
    a = _dot(h, wg_ref[...])
    b = _dot(h, wu_ref[...])
    act = (a * jax.nn.sigmoid(a) * (0.5 * b)).astype(BF16)
    o_ref[...] += _dot(act, wd_ref[...])


def _ffn(x, gain, w_gate, w_up, w_down, *, tm=1024, tf=512):
    t, d = x.shape
    f = w_gate.shape[1]
    return pl.pallas_call(
        _ffn_kernel,
        grid=(t // tm, f // tf),
        in_specs=[
            pl.BlockSpec((tm, d), lambda i, j: (i, 0)),
            pl.BlockSpec((1, d), lambda i, j: (0, 0)),
            pl.BlockSpec((d, tf), lambda i, j: (0, j)),
            pl.BlockSpec((d, tf), lambda i, j: (0, j)),
            pl.BlockSpec((tf, d), lambda i, j: (j, 0)),
        ],
        out_specs=pl.BlockSpec((tm, d), lambda i, j: (i, 0)),
        out_shape=jax.ShapeDtypeStruct((t, d), F32),
        scratch_shapes=[pltpu.VMEM((tm, d), BF16)],
        compiler_params=_params("parallel", "arbitrary"),
        name="ffn",
    )(x, gain.reshape(1, d), w_gate, w_up, w_down)


def _in_proj_kernel(x_ref, g_ref, w_ref, qg_ref, kg_ref, u_ref, o_ref, y_ref):
    h = _rms(x_ref[...], g_ref[...]).astype(BF16)
    section = lambda j: _dot(h, w_ref[:, j * SB_WIDTH:(j + 1) * SB_WIDTH])

    def head_norm(y, gain, scale, col0):
        for hd in range(SB_HEADS):
            sl = slice(hd * SB_HEAD_DIM, (hd + 1) * SB_HEAD_DIM)
            o_ref[:, col0 + sl.start:col0 + sl.stop] = (_rms(y[:, sl], gain) * scale).astype(BF16)

    y = section(0)
    for s in range(y_ref.shape[0]):
        lanes = slice(s * LANES, (s + 1) * LANES)
        y_ref[s] = y[:, lanes]
        for t in range(SSM_CHUNK):
            u_ref[t, :, lanes] = y_ref[s, pl.ds(t, u_ref.shape[1], stride=SSM_CHUNK), :].astype(BF16)
    head_norm(section(1), qg_ref[...], SB_HEAD_DIM ** -0.5 * LOG2_E, 0)
    head_norm(section(2), kg_ref[...], 1.0, SB_WIDTH)
    o_ref[:, 2 * SB_WIDTH:] = section(3).astype(BF16)


def _in_proj(x, gain, w_in, q_gain, k_gain, *, tm=512):
    t, d = x.shape
    n = w_in.shape[1]
    const = lambda shape: pl.BlockSpec(shape, lambda i: (0, 0), pipeline_mode=pl.Buffered(1))
    return pl.pallas_call(
        _in_proj_kernel,
        grid=(t // tm,),
        in_specs=[pl.BlockSpec((tm, d), lambda i: (i, 0)), const((1, d)), const((d, n)),
                  const((1, SB_HEAD_DIM)), const((1, SB_HEAD_DIM))],
        out_specs=[pl.BlockSpec((SSM_CHUNK, tm // SSM_CHUNK, SSM_WIDTH), lambda i: (0, i, 0)),
                   pl.BlockSpec((tm, n - SSM_WIDTH), lambda i: (i, 0))],
        out_shape=[jax.ShapeDtypeStruct((SSM_CHUNK, t // SSM_CHUNK, SSM_WIDTH), BF16),
                   jax.ShapeDtypeStruct((t, n - SSM_WIDTH), BF16)],
        scratch_shapes=[pltpu.VMEM((SSM_WIDTH // LANES, tm, LANES), F32)],
        compiler_params=_params("parallel"),
        name="in_proj",
    )(x, gain.reshape(1, d), w_in, q_gain.reshape(1, -1), k_gain.reshape(1, -1))


def _ssm_prep_kernel(lre_ref, lim_ref, ldt_ref, bre_ref, bim_ref, cre_ref, cim_ref, d_ref,
                     k_ref, wsr_ref, wsi_ref, cor_ref, coi_ref, zr_ref, zi_ref):
    cdim, hdim, pdim = SSM_CHUNK, SSM_GROUP, SSM_STATE
    lre = jnp.minimum(lre_ref[...], LAMBDA_RE_MAX)
    lim = lim_ref[...]
    dt = jnp.exp(ldt_ref[...])
    a = lre * dt
    th = lim * dt

    def zpow(steps):
        mag = jnp.exp(steps * a)
        return mag * jnp.cos(steps * th), mag * jnp.sin(steps * th)

    z1r, z1i = zpow(jnp.ones((1, 1), F32))
    den = lre * lre + lim * lim
    cfr = ((z1r - 1.0) * lre + z1i * lim) / den
    cfi = (z1i * lre - (z1r - 1.0) * lim) / den
    btr, bti = bre_ref[...], bim_ref[...]
    bbr = cfr * btr - cfi * bti
    bbi = cfr * bti + cfi * btr
    cr, ci = cre_ref[...], cim_ref[...]

    steps = lax.broadcasted_iota(jnp.int32, (cdim, 1), 0).astype(F32)

    def outer(zr, zi, mr, mi):
        zr, zi = zr[:, None, :], zi[:, None, :]
        mr, mi = mr[None, :, :], mi[None, :, :]
        return ((zr * mr - zi * mi).reshape(cdim * hdim, pdim),
                (zr * mi + zi * mr).reshape(cdim * hdim, pdim))

    zr, zi = zpow(steps)
    czr, czi = outer(zr, zi, cr, ci)
    kk = (_dot_nt(czr, bbr, precision=lax.Precision.HIGHEST)
          - _dot_nt(czi, bbi, precision=lax.Precision.HIGHEST))
    row = lax.broadcasted_iota(jnp.int32, kk.shape, 0)
    col = lax.broadcasted_iota(jnp.int32, kk.shape, 1)
    k_ref[...] = kk + jnp.where(row == col, d_ref[...], 0.0)

    zr, zi = zpow((cdim - 1.0) - steps)
    wsr_ref[...], wsi_ref[...] = outer(zr, zi, bbr, bbi)

    zr, zi = zpow(steps + 1.0)
    cor, coi = outer(zr, zi, cr, ci)
    cor_ref[...] = cor
    coi_ref[...] = -coi

    zr_ref[...], zi_ref[...] = zpow(jnp.full((1, 1), float(cdim), F32))


def _ssm_prep(lam_re, lam_im, b_re, b_im, c_re, c_im, log_dt, d_skip):
    g, p, h, r = SSM_GROUPS, SSM_STATE, SSM_GROUP, SSM_ROW

    def per_group(*shape):
        return pl.BlockSpec((None,) + shape, lambda i: (i,) + (0,) * len(shape))

    outs = pl.pallas_call(
        _ssm_prep_kernel,
        grid=(g,),
        in_specs=[per_group(1, p), per_group(1, p), per_group(1, 1), per_group(h, p), per_group(h, p),
                  per_group(h, p), per_group(h, p), per_group(1, h)],
        out_specs=[per_group(r, h), per_group(r, p), per_group(r, p), per_group(r, p), per_group(r, p),
                   per_group(1, p), per_group(1, p)],
        out_shape=[jax.ShapeDtypeStruct((g, r, h), F32)] + [jax.ShapeDtypeStruct((g, r, p), F32)] * 4
                  + [jax.ShapeDtypeStruct((g, 1, p), F32)] * 2,
        compiler_params=_params("parallel"),
        name="ssm_prep",
    )(lam_re.reshape(g, 1, p), lam_im.reshape(g, 1, p), log_dt.reshape(g, 1, 1),
      b_re.transpose(0, 2, 1), b_im.transpose(0, 2, 1), c_re, c_im, d_skip.reshape(g, 1, h))
    kk, wsr, wsi, cor, coi, zr, zi = outs

    c, ns, gl = SSM_CHUNK, SSM_SLABS, SLAB_GROUPS
    same = jnp.eye(gl, dtype=bool)
    kt = kk.reshape(ns, gl, c, h, h).transpose(0, 2, 1, 4, 3).astype(BF16)
    lag_blocks = jnp.where(same[None, None, :, None, :, None], kt[:, :, :, :, None, :], 0)
    lag_blocks = lag_blocks.reshape(ns, c, gl * h, gl * h)

    def per_step(m):
        m = m.reshape(ns, gl, c, h, p).transpose(0, 2, 1, 3, 4).reshape(ns, c, gl * h, p).astype(BF16)
        return jnp.concatenate([m, m], axis=3)

    st_blocks = jnp.stack([per_step(wsr), per_step(wsi)], axis=1)
    out_blocks = jnp.stack([per_step(cor), per_step(coi)], axis=1)
    return lag_blocks, st_blocks, out_blocks, zr.reshape(ns, 1, gl * p), zi.reshape(ns, 1, gl * p)


def _ssm_scan_kernel(u_ref, lag_ref, stb_ref, outb_ref, zr_ref, zi_ref, y_ref,
                     toep_ref, wst_ref, wout_ref, ye_ref, sp_ref, st_ref):
    c_dim, n_batch, n_chunks, _ = u_ref.shape
    n_s = st_ref.shape[1] // 2

    @pl.when(pl.program_id(1) == 0)
    def _():
        st_ref[...] = jnp.zeros(st_ref.shape, F32)
        row_group = lax.broadcasted_iota(jnp.int32, (LANES, n_s), 0) // SSM_GROUP
        col_group = lax.broadcasted_iota(jnp.int32, (LANES, n_s), 1) // SSM_STATE
        own_group = row_group == col_group
        for tau in range(c_dim):
            rows = slice(tau * LANES, (tau + 1) * LANES)
            for t in range(c_dim):
                blk = lag_ref[t - tau] if t >= tau else jnp.zeros((LANES, LANES), BF16)
                toep_ref[rows, t * LANES:(t + 1) * LANES] = blk
            for part in range(2):
                cols = slice(part * n_s, (part + 1) * n_s)
                tile = lambda blk: jnp.concatenate([blk] * (n_s // LANES), axis=1)
                wst_ref[rows, cols] = jnp.where(own_group, tile(stb_ref[part, tau]), 0).astype(BF16)
                wout_ref[rows, cols] = jnp.where(own_group, tile(outb_ref[part, tau]), 0).astype(BF16)

    lhs = jnp.concatenate([u_ref[t].reshape(n_batch * n_chunks, LANES) for t in range(c_dim)], axis=1)
    wide = 2 * LANES
    y_intra = [_dot(lhs[:, :(tb + 1) * wide], toep_ref[:(tb + 1) * wide, tb * wide:(tb + 1) * wide])
               for tb in range(c_dim * LANES // wide)]
    ends = _dot(lhs, wst_ref[...])
    n_v = n_s // LANES
    for k in range(2 * n_v):
        ye_ref[k] = ends[:, k * LANES:(k + 1) * LANES]
    lane_slab = lambda ref, k: ref[:, k * LANES:(k + 1) * LANES]
    zr = [jnp.broadcast_to(lane_slab(zr_ref, k), (n_batch, LANES)) for k in range(n_v)]
    zi = [jnp.broadcast_to(lane_slab(zi_ref, k), (n_batch, LANES)) for k in range(n_v)]

    def step(c, state):
        rows = pl.ds(c, n_batch, stride=n_chunks)
        new = []
        for k in range(n_v):
            sr, si = state[k], state[n_v + k]
            sp_ref[k, rows, :] = sr
            sp_ref[n_v + k, rows, :] = si
            new.append((zr[k] * sr - zi[k] * si + ye_ref[k, rows, :],
                        zr[k] * si + zi[k] * sr + ye_ref[n_v + k, rows, :]))
        return tuple(s[0] for s in new) + tuple(s[1] for s in new)

    state = lax.fori_loop(0, n_chunks, step, tuple(lane_slab(st_ref, k) for k in range(2 * n_v)), unroll=4)
    for k in range(2 * n_v):
        st_ref[:, k * LANES:(k + 1) * LANES] = state[k]
    s_prev = jnp.concatenate([sp_ref[k] for k in range(2 * n_v)], axis=1)
    y_carry = _dot_nt(s_prev.astype(BF16), wout_ref[...])
    for t in range(c_dim):
        lo = t * LANES % wide
        y_t = y_intra[t * LANES // wide][:, lo:lo + LANES] + y_carry[:, t * LANES:(t + 1) * LANES]
        for b in range(n_batch):
            y_ref[b, pl.ds(t, n_chunks, stride=c_dim), :] = y_t[b * n_chunks:(b + 1) * n_chunks]


def _ssm_scan(u_t, lag_blocks, st_blocks, out_blocks, zr, zi, *, n_batch, chunk_tile=64):
    c = SSM_CHUNK
    n_chunks = u_t.shape[1] // n_batch
    n_s = zr.shape[2]
    rows = n_batch * chunk_tile

    def per_slab(a):
        zeros = (0,) * (a.ndim - 1)
        return pl.BlockSpec((None,) + a.shape[1:], lambda s, j: (s,) + zeros, pipeline_mode=pl.Buffered(1))

    return pl.pallas_call(
        _ssm_scan_kernel,
        grid=(SSM_SLABS, n_chunks // chunk_tile),
        in_specs=[pl.BlockSpec((c, n_batch, chunk_tile, LANES), lambda s, j: (0, 0, j, s)),
                  per_slab(lag_blocks), per_slab(st_blocks), per_slab(out_blocks), per_slab(zr), per_slab(zi)],
        out_specs=pl.BlockSpec((n_batch, chunk_tile * c, LANES), lambda s, j: (0, j, s)),
        out_shape=jax.ShapeDtypeStruct((n_batch, n_chunks * c, SSM_WIDTH), F32),
        scratch_shapes=[pltpu.VMEM((c * LANES, c * LANES), BF16),
                        pltpu.VMEM((c * LANES, 2 * n_s), BF16), pltpu.VMEM((c * LANES, 2 * n_s), BF16),
                        pltpu.VMEM((2 * n_s // LANES, rows, LANES), F32),
                        pltpu.VMEM((2 * n_s // LANES, rows, LANES), F32), pltpu.VMEM((n_batch, 2 * n_s), F32)],
        compiler_params=_params("parallel", "arbitrary"),
        name="ssm_scan",
    )(u_t.reshape(c, n_batch, n_chunks, SSM_WIDTH), lag_blocks, st_blocks, out_blocks, zr, zi)


def _ssm_glu_kernel(y_ref, w_ref, b_ref, g_ref, o_ref):
    y = y_ref[...]
    z = 0.5 * y * (1.0 + jnp.tanh(math.sqrt(2.0 / math.pi) * (y + 0.044715 * (y * y * y))))
    out = z * jax.nn.sigmoid(_dot(z.astype(BF16), w_ref[...]) + b_ref[...])
    o_ref[...] = _rms(out, g_ref[...]).astype(BF16)


def _ssm_glu(y, w_glu, b_glu, gain, *, tm=1024):
    t, w = y.shape
    return pl.pallas_call(
        _ssm_glu_kernel,
        grid=(t // tm,),
        in_specs=[pl.BlockSpec((tm, w), lambda i: (i, 0)), pl.BlockSpec((w, w), lambda i: (0, 0)),
                  pl.BlockSpec((1, w), lambda i: (0, 0)), pl.BlockSpec((1, w), lambda i: (0, 0))],
        out_specs=pl.BlockSpec((tm, w), lambda i: (i, 0)),
        out_shape=jax.ShapeDtypeStruct((t, w), BF16),
        compiler_params=_params("parallel"),
        name="ssm_glu",
    )(y, w_glu, b_glu.reshape(1, w), gain.reshape(1, w))


def _sb_attn_kernel(q_ref, k_ref, v_ref, o_ref, run_ref):
    seq = q_ref.shape[0]
    n_sub = Q_SPAN // Q_TILE
    diag_blocks = Q_SPAN // K_TILE
    span_blocks = K_SPAN // K_TILE
    s_idx = lax.broadcasted_iota(jnp.int32, (K_TILE, K_TILE), 0)
    j_idx = lax.broadcasted_iota(jnp.int32, (K_TILE, K_TILE), 1)
    later = (s_idx >= j_idx).astype(BF16)
    later2 = jnp.concatenate([later, later], axis=0)
    row =lax.broadcasted_iota(jnp.int32, (Q_TILE, K_TILE), 0)
    col = lax.broadcasted_iota(jnp.int32, (Q_TILE, K_TILE), 1)

    def sweep(base, blocks):
        n, depth = len(blocks), 6
        z, mid, cs = {}, {}, {}
        for step in range(n + depth - 1):
            i = step
            if i < n:
                r0, nr, kb, _ = blocks[i]
                keys = pl.ds(pl.multiple_of(kb * K_TILE, K_TILE), K_TILE)
                z[i] = _dot_nt(q_ref[pl.ds(base + r0, nr), :], k_ref[keys, :])
            i = step - 2
            if 0 <= i < n:
                valid = blocks[i][3]
                zi = z.pop(i)
                softplus = jnp.maximum(zi, 0.0) + jnp.log(1.0 + jnp.exp2(-jnp.abs(zi))) * LOG2_E
                drop = softplus if valid is None else jnp.where(valid, softplus, 0.0)
                hi = drop.astype(BF16)
                lo = (drop - hi.astype(F32)).astype(BF16)
                mid[i] = (zi, jnp.concatenate([hi, lo], axis=1))
            i = step - 3
            if 0 <= i < n:
                cs[i] = _dot(mid[i][1], later2)
            i = step - 5
            if 0 <= i < n:
                r0, nr, kb, valid = blocks[i]
                keys = pl.ds(pl.multiple_of(kb * K_TILE, K_TILE), K_TILE)
                zi, _ = mid.pop(i)
                csi = cs.pop(i)
                total = csi[:, :1]
                run = run_ref[r0:r0 + nr, :]
                w = jnp.exp2(zi - (run + csi))
                if valid is not None:
                    w = jnp.where(valid, w, 0.0)
                o_ref[pl.ds(base + r0, nr), :] += _dot(w.astype(BF16), v_ref[keys, :])
                run_ref[r0:r0 + nr, :] = run + total

    def q_span(qs, _):
        base = pl.multiple_of(qs * Q_SPAN, Q_SPAN)
        kb_diag = qs * diag_blocks
        o_ref[pl.ds(base, Q_SPAN), :] = jnp.zeros((Q_SPAN, SB_HEAD_DIM), F32)
        run_ref[...] = jnp.zeros(run_ref.shape, F32)
        diag = []
        for d in range(diag_blocks):
            for a in range(n_sub):
                top = (a * Q_TILE) // K_TILE
                if d == 0:
                    diag.append((a * Q_TILE, Q_TILE, kb_diag + top, top * K_TILE + col < a * Q_TILE + row))
                elif top - d >= 0:
                    diag.append((a * Q_TILE, Q_TILE, kb_diag + top - d, None))
        sweep(base, diag)

        def below(n, _):
            kb_hi = kb_diag - 1 - n * span_blocks
            sweep(base, [(a * Q_TILE, Q_TILE, kb_hi - d, None) for d in range(span_blocks) for a in range(n_sub)])
            return 0

        lax.fori_loop(0, qs * (Q_SPAN // K_SPAN), below, 0)
        return 0

    lax.fori_loop(0, seq // Q_SPAN, q_span, 0)


def _sb_attn(qkv, *, n_batch, seq):
    t = qkv.shape[0]
    blk = lambda off: pl.BlockSpec((seq, SB_HEAD_DIM), lambda b, h: (b, off + h))
    return pl.pallas_call(
        _sb_attn_kernel,
        grid=(n_batch, SB_HEADS),
        in_specs=[blk(0), blk(SB_HEADS), blk(2 * SB_HEADS)],
        out_specs=blk(0),
        out_shape=jax.ShapeDtypeStruct((t, SB_WIDTH), F32),
        scratch_shapes=[pltpu.VMEM((Q_SPAN, 1), F32)],
        compiler_params=_params("parallel", "parallel"),
        name="sb_attn",
    )(qkv, qkv, qkv)


def _out_proj_kernel(x_ref, ms_ref, sb_ref, g_ref, w_ref, o_ref):
    sb = _rms(sb_ref[...], g_ref[...]).astype(BF16)
    o_ref[...] = (x_ref[...] + _dot(ms_ref[...], w_ref[:SSM_WIDTH, :]) + _dot(sb, w_ref[SSM_WIDTH:, :]))


def _out_proj(x, mixed_ssm, y_sb, sb_gain, w_out, *, tm=512):
    t, d = x.shape
    row = lambda w: pl.BlockSpec((tm, w), lambda i: (i, 0))
    return pl.pallas_call(
        _out_proj_kernel,
        grid=(t // tm,),
        in_specs=[row(d), row(SSM_WIDTH), row(SB_WIDTH), pl.BlockSpec((1, SB_WIDTH), lambda i: (0, 0)),
                  pl.BlockSpec(w_out.shape, lambda i: (0, 0))],
        out_specs=row(d),
        out_shape=jax.ShapeDtypeStruct((t, d), F32),
        compiler_params=_params("parallel"),
        name="out_proj",
    )(x, mixed_ssm, y_sb, sb_gain.reshape(1, -1), w_out)


def _ple_kernel(x_ref, p_ref, g_ref, wg_ref, wp_ref, pg_ref, o_ref):
    x = x_ref[...]
    gate = jax.nn.sigmoid(_dot(_rms(x, g_ref[...]).astype(BF16), wg_ref[...]))
    e = _dot(p_ref[...].astype(BF16), wp_ref[...]) * gate
    o_ref[...] = x + _rms(e, pg_ref[...])


def _ple(x, p, gain, w_gate, w_proj, post_gain, *, tm=512):
    t, d = x.shape
    row = lambda w: pl.BlockSpec((tm, w), lambda i: (i, 0))
    vec = pl.BlockSpec((1, d), lambda i: (0, 0))
    return pl.pallas_call(
        _ple_kernel,
        grid=(t // tm,),
        in_specs=[row(d), row(p.shape[1]), vec, pl.BlockSpec(w_gate.shape, lambda i: (0, 0)),
                  pl.BlockSpec(w_proj.shape, lambda i: (0, 0)), vec],
        out_specs=row(d),
        out_shape=jax.ShapeDtypeStruct((t, d), F32),
        compiler_params=_params("parallel"),
        name="ple",
    )(x, p, gain.reshape(1, d), w_gate, w_proj, post_gain.reshape(1, d))


def _s5_mixer(u, n_batch, seq, lam_re, lam_im, b_re, b_im, c_re, c_im, log_dt, d_skip, w_glu, b_glu, gain):
    mats = _ssm_prep(lam_re, lam_im, b_re, b_im, c_re, c_im, log_dt, d_skip)
    y = _ssm_scan(u, *mats, n_batch=n_batch)
    return _ssm_glu(y.reshape(n_batch * seq, SSM_WIDTH), w_glu.astype(BF16), b_glu, gain)


def kernel(x, p, ffn1_norm, ffn1_w_gate, ffn1_w_up, ffn1_w_down, mix_norm, w_in, ssm_lambda_re, ssm_lambda_im, ssm_b_re, ssm_b_im, ssm_c_re, ssm_c_im, ssm_log_dt, ssm_d, ssm_w_glu, ssm_b_glu, q_norm, k_norm, out_norm_ssm, out_norm_sb, w_out, ffn2_norm, ffn2_w_gate, ffn2_w_up, ffn2_w_down, ple_norm, w_ple_gate, w_ple_proj, ple_post_norm):
    n_batch, seq, d = x.shape
    xt = x.reshape(n_batch * seq, d)
    for i in range(p.shape[0]):
        xt = _ffn(xt, ffn1_norm[i], ffn1_w_gate[i].astype(BF16), ffn1_w_up[i].astype(BF16),
                  ffn1_w_down[i].astype(BF16))
        u, qkv = _in_proj(xt, mix_norm[i], w_in[i].astype(BF16), q_norm[i], k_norm[i])
        mixed_ssm = _s5_mixer(u, n_batch, seq, ssm_lambda_re[i], ssm_lambda_im[i],
                              ssm_b_re[i], ssm_b_im[i], ssm_c_re[i], ssm_c_im[i], ssm_log_dt[i], ssm_d[i],
                              ssm_w_glu[i], ssm_b_glu[i], out_norm_ssm[i])
        y_sb = _sb_attn(qkv, n_batch=n_batch, seq=seq)
        xt = _out_proj(xt, mixed_ssm, y_sb, out_norm_sb[i], w_out[i].astype(BF16))
        xt = _ffn(xt, ffn2_norm[i], ffn2_w_gate[i].astype(BF16), ffn2_w_up[i].astype(BF16),
                  ffn2_w_down[i].astype(BF16))
        xt = _ple(xt, p[i].reshape(n_batch * seq, -1), ple_norm[i], w_ple_gate[i].astype(BF16),
                  w_ple_proj[i].astype(BF16), ple_post_norm[i])
    return xt.reshape(n_batch, seq, d)
```

```python
import math

import jax
import jax.numpy as jnp
from jax import lax
from jax.experimental import pallas as pl
from jax.experimental.pallas import tpu as pltpu

F32 = jnp.float32
BF16 = jnp.bfloat16

D_MODEL = 2048
PLE_DIM = 256
SSM_WIDTH = 1024
SSM_GROUP = 16
SSM_GROUPS = SSM_WIDTH // SSM_GROUP
SSM_STATE = 64
SB_WIDTH = 1024
SB_HEAD_DIM = 128
SB_HEADS = SB_WIDTH // SB_HEAD_DIM
D_FF = 5632
EPS = 1e-6
LAMBDA_RE_MAX = -1e-4
LOG2_E = math.log2(math.e)

SSM_CHUNK = 16
SSM_ROW = SSM_CHUNK * SSM_GROUP
LANES = 128
SLAB_GROUPS = LANES // SSM_GROUP
SSM_SLABS = SSM_WIDTH // LANES
Q_TILE = 256
K_TILE = 256
Q_SPAN = 1024
K_SPAN = 1024
VMEM_LIMIT = 56 * 1024 * 1024


def _params(*semantics):
    return pltpu.CompilerParams(dimension_semantics=semantics, vmem_limit_bytes=VMEM_LIMIT)


def _rms(x, gain):
    return x * lax.rsqrt(jnp.mean(x * x, axis=-1, keepdims=True) + EPS) * gain


def _dot(a, b):
    return jnp.dot(a, b, preferred_element_type=F32)


def _dot_nt(a, b, **kw):
    return lax.dot_general(a, b, (((1,), (1,)), ((), ())), preferred_element_type=F32, **kw)


def _ffn_kernel(x_ref, g_ref, wg_ref, wu_ref, wd_ref, o_ref, h_ref):
    j = pl.program_id(1)

    @pl.when(j == 0)
    def _():
        x = x_ref[...]
        h_ref[...] = _rms(x, g_ref[...]).astype(BF16)
        o_ref[...] = x

    h = h_ref[...]
    a = _dot(h, wg_ref[...])
    b = _dot(h, wu_ref[...])
    act = (a * jax.nn.sigmoid(a) * (0.5 * b)).astype(BF16)
    o_ref[...] += _dot(act, wd_ref[...])


def _ffn(x, gain, w_gate, w_up, w_down, *, tm=1024, tf=512):
    t, d = x.shape
    f = w_gate.shape[1]
    return pl.pallas_call(
        _ffn_kernel,
        grid=(t // tm, f // tf),
        in_specs=[
            pl.BlockSpec((tm, d), lambda i, j: (i, 0)),
            pl.BlockSpec((1, d), lambda i, j: (0, 0)),
            pl.BlockSpec((d, tf), lambda i, j: (0, j)),
            pl.BlockSpec((d, tf), lambda i, j: (0, j)),
            pl.BlockSpec((tf, d), lambda i, j: (j, 0)),
        ],
        out_specs=pl.BlockSpec((tm, d), lambda i, j: (i, 0)),
        out_shape=jax.ShapeDtypeStruct((t, d), F32),
        scratch_shapes=[pltpu.VMEM((tm, d), BF16)],
        compiler_params=_params("parallel", "arbitrary"),
        name="ffn",
    )(x, gain.reshape(1, d), w_gate, w_up, w_down)


def _in_proj_kernel(x_ref, g_ref, w_ref, qg_ref, kg_ref, u_ref, o_ref, y_ref):
    h = _rms(x_ref[...], g_ref[...]).astype(BF16)
    section = lambda j: _dot(h, w_ref[:, j * SB_WIDTH:(j + 1) * SB_WIDTH])

    def head_norm(y, gain, scale, col0):
        for hd in range(SB_HEADS):
            sl = slice(hd * SB_HEAD_DIM, (hd + 1) * SB_HEAD_DIM)
            o_ref[:, col0 + sl.start:col0 + sl.stop] = (_rms(y[:, sl], gain) * scale).astype(BF16)

    y = section(0)
    for s in range(y_ref.shape[0]):
        lanes = slice(s * LANES, (s + 1) * LANES)
        y_ref[s] = y[:, lanes]
        for t in range(SSM_CHUNK):
            u_ref[t, :, lanes] = y_ref[s, pl.ds(t, u_ref.shape[1], stride=SSM_CHUNK), :].astype(BF16)
    head_norm(section(1), qg_ref[...], SB_HEAD_DIM ** -0.5 * LOG2_E, 0)
    head_norm(section(2), kg_ref[...], 1.0, SB_WIDTH)
    o_ref[:, 2 * SB_WIDTH:] = section(3).astype(BF16)


def _in_proj(x, gain, w_in, q_gain, k_gain, *, tm=512):
    t, d = x.shape
    n = w_in.shape[1]
    const = lambda shape: pl.BlockSpec(shape, lambda i: (0, 0), pipeline_mode=pl.Buffered(1))
    return pl.pallas_call(
        _in_proj_kernel,
        grid=(t // tm,),
        in_specs=[pl.BlockSpec((tm, d), lambda i: (i, 0)), const((1, d)), const((d, n)),
                  const((1, SB_HEAD_DIM)), const((1, SB_HEAD_DIM))],
        out_specs=[pl.BlockSpec((SSM_CHUNK, tm // SSM_CHUNK, SSM_WIDTH), lambda i: (0, i, 0)),
                   pl.BlockSpec((tm, n - SSM_WIDTH), lambda i: (i, 0))],
        out_shape=[jax.ShapeDtypeStruct((SSM_CHUNK, t // SSM_CHUNK, SSM_WIDTH), BF16),
                   jax.ShapeDtypeStruct((t, n - SSM_WIDTH), BF16)],
        scratch_shapes=[pltpu.VMEM((SSM_WIDTH // LANES, tm, LANES), F32)],
        compiler_params=_params("parallel"),
        name="in_proj",
    )(x, gain.reshape(1, d), w_in, q_gain.reshape(1, -1), k_gain.reshape(1, -1))


def _ssm_prep_kernel(*refs):
    for gi in range(refs[0].shape[0]):
        _ssm_prep_group(*[r.at[gi] for r in refs])


def _ssm_prep_group(lre_ref, lim_ref, ldt_ref, bre_ref, bim_ref, cre_ref, cim_ref, d_ref,
                    k_ref, wsr_ref, wsi_ref, cor_ref, coi_ref, zr_ref, zi_ref):
    cdim, hdim, pdim = SSM_CHUNK, SSM_GROUP, SSM_STATE
    lre = jnp.minimum(lre_ref[...], LAMBDA_RE_MAX)
    lim = lim_ref[...]
    dt = jnp.exp(ldt_ref[...])
    a = lre * dt
    th = lim * dt

    def zpow(steps):
        mag = jnp.exp(steps * a)
        return mag * jnp.cos(steps * th), mag * jnp.sin(steps * th)

    z1r, z1i = zpow(jnp.ones((1, 1), F32))
    den = lre * lre + lim * lim
    cfr = ((z1r - 1.0) * lre + z1i * lim) / den
    cfi = (z1i * lre - (z1r - 1.0) * lim) / den
    btr, bti = bre_ref[...], bim_ref[...]
    bbr = cfr * btr - cfi * bti
    bbi = cfr * bti + cfi * btr
    cr, ci = cre_ref[...], cim_ref[...]

    steps = lax.broadcasted_iota(jnp.int32, (cdim, 1), 0).astype(F32)

    def outer(zr, zi, mr, mi):
        zr, zi = zr[:, None, :], zi[:, None, :]
        mr, mi = mr[None, :, :], mi[None, :, :]
        return ((zr * mr - zi * mi).reshape(cdim * hdim, pdim),
                (zr * mi + zi * mr).reshape(cdim * hdim, pdim))

    zr, zi = zpow(steps)
    czr, czi = outer(zr, zi, cr, ci)
    kk = (_dot_nt(czr, bbr, precision=lax.Precision.HIGHEST)
          - _dot_nt(czi, bbi, precision=lax.Precision.HIGHEST))
    row = lax.broadcasted_iota(jnp.int32, kk.shape, 0)
    col = lax.broadcasted_iota(jnp.int32, kk.shape, 1)
    k_ref[...] = kk + jnp.where(row == col, d_ref[...], 0.0)

    zr, zi = zpow((cdim - 1.0) - steps)
    wsr_ref[...], wsi_ref[...] = outer(zr, zi, bbr, bbi)

    zr, zi = zpow(steps + 1.0)
    cor, coi = outer(zr, zi, cr, ci)
    cor_ref[...] = cor
    coi_ref[...] = -coi

    zr_ref[...], zi_ref[...] = zpow(jnp.full((1, 1), float(cdim), F32))


def _ssm_prep(lam_re, lam_im, b_re, b_im, c_re, c_im, log_dt, d_skip):
    g, p, h, r = SSM_GROUPS, SSM_STATE, SSM_GROUP, SSM_ROW

    def per_group(*shape):
        return pl.BlockSpec((SLAB_GROUPS,) + shape, lambda i: (i,) + (0,) * len(shape))

    outs = pl.pallas_call(
        _ssm_prep_kernel,
        grid=(g // SLAB_GROUPS,),
        in_specs=[per_group(1, p), per_group(1, p), per_group(1, 1), per_group(h, p), per_group(h, p),
                  per_group(h, p), per_group(h, p), per_group(1, h)],
        out_specs=[per_group(r, h), per_group(r, p), per_group(r, p), per_group(r, p), per_group(r, p),
                   per_group(1, p), per_group(1, p)],
        out_shape=[jax.ShapeDtypeStruct((g, r, h), F32)] + [jax.ShapeDtypeStruct((g, r, p), F32)] * 4
                  + [jax.ShapeDtypeStruct((g, 1, p), F32)] * 2,
        compiler_params=_params("parallel"),
        name="ssm_prep",
    )(lam_re.reshape(g, 1, p), lam_im.reshape(g, 1, p), log_dt.reshape(g, 1, 1),
      b_re.transpose(0, 2, 1), b_im.transpose(0, 2, 1), c_re, c_im, d_skip.reshape(g, 1, h))
    kk, wsr, wsi, cor, coi, zr, zi = outs

    c, ns, gl = SSM_CHUNK, SSM_SLABS, SLAB_GROUPS
    same = jnp.eye(gl, dtype=bool)
    kt = kk.reshape(ns, gl, c, h, h).transpose(0, 2, 1, 4, 3).astype(BF16)
    lag_blocks = jnp.where(same[None, None, :, None, :, None], kt[:, :, :, :, None, :], 0)
    lag_blocks = lag_blocks.reshape(ns, c, gl * h, gl * h)

    def per_step(m):
        m = m.reshape(ns, gl, c, h, p).transpose(0, 2, 1, 3, 4).reshape(ns, c, gl * h, p).astype(BF16)
        return jnp.concatenate([m, m], axis=3)

    st_blocks = jnp.stack([per_step(wsr), per_step(wsi)], axis=1)
    out_blocks = jnp.stack([per_step(cor), per_step(coi)], axis=1)
    return lag_blocks, st_blocks, out_blocks, zr.reshape(ns, 1, gl * p), zi.reshape(ns, 1, gl * p)


def _ssm_scan_kernel(u_ref, lag_ref, stb_ref, outb_ref, zr_ref, zi_ref, y_ref,
                     toep_ref, wst_ref, wout_ref, ye_ref, sp_ref, st_ref):
    c_dim, n_batch, n_chunks, _ = u_ref.shape
    n_s = st_ref.shape[1] // 2

    @pl.when(pl.program_id(1) == 0)
    def _():
        st_ref[...] = jnp.zeros(st_ref.shape, F32)
        row_group = lax.broadcasted_iota(jnp.int32, (LANES, n_s), 0) // SSM_GROUP
        col_group = lax.broadcasted_iota(jnp.int32, (LANES, n_s), 1) // SSM_STATE
        own_group = row_group == col_group
        for tau in range(c_dim):
            rows = slice(tau * LANES, (tau + 1) * LANES)
            for t in range(c_dim):
                blk = lag_ref[t - tau] if t >= tau else jnp.zeros((LANES, LANES), BF16)
                toep_ref[rows, t * LANES:(t + 1) * LANES] = blk
            for part in range(2):
                cols = slice(part * n_s, (part + 1) * n_s)
                tile = lambda blk: jnp.concatenate([blk] * (n_s // LANES), axis=1)
                wst_ref[rows, cols] = jnp.where(own_group, tile(stb_ref[part, tau]), 0).astype(BF16)
                wout_ref[rows, cols] = jnp.where(own_group, tile(outb_ref[part, tau]), 0).astype(BF16)

    lhs = jnp.concatenate([u_ref[t].reshape(n_batch * n_chunks, LANES) for t in range(c_dim)], axis=1)
    wide = 2 * LANES
    y_intra = [_dot(lhs[:, :(tb + 1) * wide], toep_ref[:(tb + 1) * wide, tb * wide:(tb + 1) * wide])
               for tb in range(c_dim * LANES // wide)]
    ends = _dot(lhs, wst_ref[...])
    n_v = n_s // LANES
    for k in range(2 * n_v):
        ye_ref[k] = ends[:, k * LANES:(k + 1) * LANES]
    lane_slab = lambda ref, k: ref[:, k * LANES:(k + 1) * LANES]
    zr = [jnp.broadcast_to(lane_slab(zr_ref, k), (n_batch, LANES)) for k in range(n_v)]
    zi = [jnp.broadcast_to(lane_slab(zi_ref, k), (n_batch, LANES)) for k in range(n_v)]

    def step(c, state):
        rows = pl.ds(c, n_batch, stride=n_chunks)
        new = []
        for k in range(n_v):
            sr, si = state[k], state[n_v + k]
            sp_ref[k, rows, :] = sr
            sp_ref[n_v + k, rows, :] = si
            new.append((zr[k] * sr - zi[k] * si + ye_ref[k, rows, :],
                        zr[k] * si + zi[k] * sr + ye_ref[n_v + k, rows, :]))
        return tuple(s[0] for s in new) + tuple(s[1] for s in new)

    state = lax.fori_loop(0, n_chunks, step, tuple(lane_slab(st_ref, k) for k in range(2 * n_v)), unroll=4)
    for k in range(2 * n_v):
        st_ref[:, k * LANES:(k + 1) * LANES] = state[k]
    s_prev = jnp.concatenate([sp_ref[k] for k in range(2 * n_v)], axis=1)
    y_carry = _dot_nt(s_prev.astype(BF16), wout_ref[...])
    for t in range(c_dim):
        lo = t * LANES % wide
        y_t = y_intra[t * LANES // wide][:, lo:lo + LANES] + y_carry[:, t * LANES:(t + 1) * LANES]
        for b in range(n_batch):
            y_ref[b, pl.ds(t, n_chunks, stride=c_dim), :] = y_t[b * n_chunks:(b + 1) * n_chunks]


def _ssm_scan(u_t, lag_blocks, st_blocks, out_blocks, zr, zi, *, n_batch, chunk_tile=64):
    c = SSM_CHUNK
    n_chunks = u_t.shape[1] // n_batch
    n_s = zr.shape[2]
    rows = n_batch * chunk_tile

    def per_slab(a):
        zeros = (0,) * (a.ndim - 1)
        return pl.BlockSpec((None,) + a.shape[1:], lambda s, j: (s,) + zeros, pipeline_mode=pl.Buffered(1))

    return pl.pallas_call(
        _ssm_scan_kernel,
        grid=(SSM_SLABS, n_chunks // chunk_tile),
        in_specs=[pl.BlockSpec((c, n_batch, chunk_tile, LANES), lambda s, j: (0, 0, j, s)),
                  per_slab(lag_blocks), per_slab(st_blocks), per_slab(out_blocks), per_slab(zr), per_slab(zi)],
        out_specs=pl.BlockSpec((n_batch, chunk_tile * c, LANES), lambda s, j: (0, j, s)),
        out_shape=jax.ShapeDtypeStruct((n_batch, n_chunks * c, SSM_WIDTH), F32),
        scratch_shapes=[pltpu.VMEM((c * LANES, c * LANES), BF16),
                        pltpu.VMEM((c * LANES, 2 * n_s), BF16), pltpu.VMEM((c * LANES, 2 * n_s), BF16),
                        pltpu.VMEM((2 * n_s // LANES, rows, LANES), F32),
                        pltpu.VMEM((2 * n_s // LANES, rows, LANES), F32), pltpu.VMEM((n_batch, 2 * n_s), F32)],
        compiler_params=_params("parallel", "arbitrary"),
        name="ssm_scan",
    )(u_t.reshape(c, n_batch, n_chunks, SSM_WIDTH), lag_blocks, st_blocks, out_blocks, zr, zi)


def _ssm_glu_kernel(y_ref, w_ref, b_ref, g_ref, o_ref):
    y = y_ref[...]
    z = 0.5 * y * (1.0 + jnp.tanh(math.sqrt(2.0 / math.pi) * (y + 0.044715 * (y * y * y))))
    out = z * jax.nn.sigmoid(_dot(z.astype(BF16), w_ref[...]) + b_ref[...])
    o_ref[...] = _rms(out, g_ref[...]).astype(BF16)


def _ssm_glu(y, w_glu, b_glu, gain, *, tm=1024):
    t, w = y.shape
    return pl.pallas_call(
        _ssm_glu_kernel,
        grid=(t // tm,),
        in_specs=[pl.BlockSpec((tm, w), lambda i: (i, 0)), pl.BlockSpec((w, w), lambda i: (0, 0)),
                  pl.BlockSpec((1, w), lambda i: (0, 0)), pl.BlockSpec((1, w), lambda i: (0, 0))],
        out_specs=pl.BlockSpec((tm, w), lambda i: (i, 0)),
        out_shape=jax.ShapeDtypeStruct((t, w), BF16),
        compiler_params=_params("parallel"),
        name="ssm_glu",
    )(y, w_glu, b_glu.reshape(1, w), gain.reshape(1, w))


def _sb_attn_kernel(q_ref, k_ref, v_ref, o_ref, run_ref):
    seq = q_ref.shape[0]
    n_sub = Q_SPAN // Q_TILE
    diag_blocks = Q_SPAN // K_TILE
    span_blocks = K_SPAN // K_TILE
    s_idx = lax.broadcasted_iota(jnp.int32, (K_TILE, K_TILE), 0)
    j_idx = lax.broadcasted_iota(jnp.int32, (K_TILE, K_TILE), 1)
    later = (s_idx >= j_idx).astype(BF16)
    later2 = jnp.concatenate([later, later], axis=0)
    row =lax.broadcasted_iota(jnp.int32, (Q_TILE, K_TILE), 0)
    col = lax.broadcasted_iota(jnp.int32, (Q_TILE, K_TILE), 1)

    def sweep(base, blocks):
        n, depth = len(blocks), 6
        z, mid, cs = {}, {}, {}
        for step in range(n + depth - 1):
            i = step
            if i < n:
                r0, nr, kb, _ = blocks[i]
                keys = pl.ds(pl.multiple_of(kb * K_TILE, K_TILE), K_TILE)
                z[i] = _dot_nt(q_ref[pl.ds(base + r0, nr), :], k_ref[keys, :])
            i = step - 2
            if 0 <= i < n:
                valid = blocks[i][3]
                zi = z.pop(i)
                softplus = jnp.maximum(zi, 0.0) + jnp.log(1.0 + jnp.exp2(-jnp.abs(zi))) * LOG2_E
                drop = softplus if valid is None else jnp.where(valid, softplus, 0.0)
                hi = drop.astype(BF16)
                lo = (drop - hi.astype(F32)).astype(BF16)
                mid[i] = (zi, jnp.concatenate([hi, lo], axis=1))
            i = step - 3
            if 0 <= i < n:
                cs[i] = _dot(mid[i][1], later2)
            i = step - 5
            if 0 <= i < n:
                r0, nr, kb, valid = blocks[i]
                keys = pl.ds(pl.multiple_of(kb * K_TILE, K_TILE), K_TILE)
                zi, _ = mid.pop(i)
                csi = cs.pop(i)
                total = csi[:, :1]
                run = run_ref[r0:r0 + nr, :]
                w = jnp.exp2(zi - (run + csi))
                if valid is not None:
                    w = jnp.where(valid, w, 0.0)
                o_ref[pl.ds(base + r0, nr), :] += _dot(w.astype(BF16), v_ref[keys, :])
                run_ref[r0:r0 + nr, :] = run + total

    def q_span(qs, _):
        base = pl.multiple_of(qs * Q_SPAN, Q_SPAN)
        kb_diag = qs * diag_blocks
        o_ref[pl.ds(base, Q_SPAN), :] = jnp.zeros((Q_SPAN, SB_HEAD_DIM), F32)
        run_ref[...] = jnp.zeros(run_ref.shape, F32)
        diag = []
        for d in range(diag_blocks):
            for a in range(n_sub):
                top = (a * Q_TILE) // K_TILE
                if d == 0:
                    diag.append((a * Q_TILE, Q_TILE, kb_diag + top, top * K_TILE + col < a * Q_TILE + row))
                elif top - d >= 0:
                    diag.append((a * Q_TILE, Q_TILE, kb_diag + top - d, None))
        sweep(base, diag)

        def below(n, _):
            kb_hi = kb_diag - 1 - n * span_blocks
            sweep(base, [(a * Q_TILE, Q_TILE, kb_hi - d, None) for d in range(span_blocks) for a in range(n_sub)])
            return 0

        lax.fori_loop(0, qs * (Q_SPAN // K_SPAN), below, 0)
        return 0

    lax.fori_loop(0, seq // Q_SPAN, q_span, 0)


def _sb_attn(qkv, *, n_batch, seq):
    t = qkv.shape[0]
    blk = lambda off: pl.BlockSpec((seq, SB_HEAD_DIM), lambda b, h: (b, off + h))
    return pl.pallas_call(
        _sb_attn_kernel,
        grid=(n_batch, SB_HEADS),
        in_specs=[blk(0), blk(SB_HEADS), blk(2 * SB_HEADS)],
        out_specs=blk(0),
        out_shape=jax.ShapeDtypeStruct((t, SB_WIDTH), F32),
        scratch_shapes=[pltpu.VMEM((Q_SPAN, 1), F32)],
        compiler_params=_params("parallel", "parallel"),
        name="sb_attn",
    )(qkv, qkv, qkv)


def _out_proj_kernel(x_ref, ms_ref, sb_ref, g_ref, w_ref, o_ref):
    sb = _rms(sb_ref[...], g_ref[...]).astype(BF16)
    o_ref[...] = (x_ref[...] + _dot(ms_ref[...], w_ref[:SSM_WIDTH, :]) + _dot(sb, w_ref[SSM_WIDTH:, :]))


def _out_proj(x, mixed_ssm, y_sb, sb_gain, w_out, *, tm=512):
    t, d = x.shape
    row = lambda w: pl.BlockSpec((tm, w), lambda i: (i, 0))
    return pl.pallas_call(
        _out_proj_kernel,
        grid=(t // tm,),
        in_specs=[row(d), row(SSM_WIDTH), row(SB_WIDTH), pl.BlockSpec((1, SB_WIDTH), lambda i: (0, 0)),
                  pl.BlockSpec(w_out.shape, lambda i: (0, 0))],
        out_specs=row(d),
        out_shape=jax.ShapeDtypeStruct((t, d), F32),
        compiler_params=_params("parallel"),
        name="out_proj",
    )(x, mixed_ssm, y_sb, sb_gain.reshape(1, -1), w_out)


def _ple_kernel(x_ref, p_ref, g_ref, wg_ref, wp_ref, pg_ref, o_ref):
    x = x_ref[...]
    gate = jax.nn.sigmoid(_dot(_rms(x, g_ref[...]).astype(BF16), wg_ref[...]))
    e = _dot(p_ref[...].astype(BF16), wp_ref[...]) * gate
    o_ref[...] = x + _rms(e, pg_ref[...])


def _ple(x, p, gain, w_gate, w_proj, post_gain, *, tm=512):
    t, d = x.shape
    row = lambda w: pl.BlockSpec((tm, w), lambda i: (i, 0))
    vec = pl.BlockSpec((1, d), lambda i: (0, 0))
    return pl.pallas_call(
        _ple_kernel,
        grid=(t // tm,),
        in_specs=[row(d), row(p.shape[1]), vec, pl.BlockSpec(w_gate.shape, lambda i: (0, 0)),
                  pl.BlockSpec(w_proj.shape, lambda i: (0, 0)), vec],
        out_specs=row(d),
        out_shape=jax.ShapeDtypeStruct((t, d), F32),
        compiler_params=_params("parallel"),
        name="ple",
    )(x, p, gain.reshape(1, d), w_gate, w_proj, post_gain.reshape(1, d))


def _s5_mixer(u, n_batch, seq, lam_re, lam_im, b_re, b_im, c_re, c_im, log_dt, d_skip, w_glu, b_glu, gain):
    mats = _ssm_prep(lam_re, lam_im, b_re, b_im, c_re, c_im, log_dt, d_skip)
    y = _ssm_scan(u, *mats, n_batch=n_batch)
    return _ssm_glu(y.reshape(n_batch * seq, SSM_WIDTH), w_glu.astype(BF16), b_glu, gain)


def kernel(x, p, ffn1_norm, ffn1_w_gate, ffn1_w_up, ffn1_w_down, mix_norm, w_in, ssm_lambda_re, ssm_lambda_im, ssm_b_re, ssm_b_im, ssm_c_re, ssm_c_im, ssm_log_dt, ssm_d, ssm_w_glu, ssm_b_glu, q_norm, k_norm, out_norm_ssm, out_norm_sb, w_out, ffn2_norm, ffn2_w_gate, ffn2_w_up, ffn2_w_down, ple_norm, w_ple_gate, w_ple_proj, ple_post_norm):
    n_batch, seq, d = x.shape
    xt = x.reshape(n_batch * seq, d)
    for i in range(p.shape[0]):
        xt = _ffn(xt, ffn1_norm[i], ffn1_w_gate[i].astype(BF16), ffn1_w_up[i].astype(BF16),
                  ffn1_w_down[i].astype(BF16))
        u, qkv = _in_proj(xt, mix_norm[i], w_in[i].astype(BF16), q_norm[i], k_norm[i])
        mixed_ssm = _s5_mixer(u, n_batch, seq, ssm_lambda_re[i], ssm_lambda_im[i],
                              ssm_b_re[i], ssm_b_im[i], ssm_c_re[i], ssm_c_im[i], ssm_log_dt[i], ssm_d[i],
                              ssm_w_glu[i], ssm_b_glu[i], out_norm_ssm[i])
        y_sb = _sb_attn(qkv, n_batch=n_batch, seq=seq)
        xt = _out_proj(xt, mixed_ssm, y_sb, out_norm_sb[i], w_out[i].astype(BF16))
        xt = _ffn(xt, ffn2_norm[i], ffn2_w_gate[i].astype(BF16), ffn2_w_up[i].astype(BF16),
                  ffn2_w_down[i].astype(BF16))
        xt = _ple(xt, p[i].reshape(n_batch * seq, -1), ple_norm[i], w_ple_gate[i].astype(BF16),
                  w_ple_proj[i].astype(BF16), ple_post_norm[i])
    return xt.reshape(n_batch, seq, d)
```

```python
import math

import jax
import jax.numpy as jnp
from jax import lax
from jax.experimental import pallas as pl
from jax.experimental.pallas import tpu as pltpu

F32 = jnp.float32
BF16 = jnp.bfloat16

D_MODEL = 2048
PLE_DIM = 256
SSM_WIDTH = 1024
SSM_GROUP = 16
SSM_GROUPS = SSM_WIDTH // SSM_GROUP
SSM_STATE = 64
SB_WIDTH = 1024
SB_HEAD_DIM = 128
SB_HEADS = SB_WIDTH // SB_HEAD_DIM
D_FF = 5632
EPS = 1e-6
LAMBDA_RE_MAX = -1e-4
LOG2_E = math.log2(math.e)

SSM_CHUNK = 16
SSM_ROW = SSM_CHUNK * SSM_GROUP
LANES = 128
SLAB_GROUPS = LANES // SSM_GROUP
SSM_SLABS = SSM_WIDTH // LANES
Q_TILE = 256
K_TILE = 256
Q_SPAN = 1024
K_SPAN = 1024
VMEM_LIMIT = 56 * 1024 * 1024


def _params(*semantics):
    return pltpu.CompilerParams(dimension_semantics=semantics, vmem_limit_bytes=VMEM_LIMIT)


def _rms(x, gain):
    return x * lax.rsqrt(jnp.mean(x * x, axis=-1, keepdims=True) + EPS) * gain


def _dot(a, b):
    return jnp.dot(a, b, preferred_element_type=F32)


def _dot_nt(a, b, **kw):
    return lax.dot_general(a, b, (((1,), (1,)), ((), ())), preferred_element_type=F32, **kw)


def _ffn_kernel(x_ref, g_ref, wg_ref, wu_ref, wd_ref, *rest):
    n_cast = (len(rest) - 2) // 2
    o_ref, h_ref = rest[n_cast], rest[-1]
    for src, dst in zip(rest[:n_cast], rest[n_cast + 1:-1]):
        dst[...] = src[...].astype(BF16)
    j = pl.program_id(1)

    @pl.when(j == 0)
    def _():
        x = x_ref[...]
        h_ref[...] = _rms(x, g_ref[...]).astype(BF16)
        o_ref[...] = x

    h = h_ref[...]
    a = _dot(h, wg_ref[...])
    b = _dot(h, wu_ref[...])
    act = (a * jax.nn.sigmoid(a) * (0.5 * b)).astype(BF16)
    o_ref[...] += _dot(act, wd_ref[...])


def _ffn(x, gain, w_gate, w_up, w_down, *, cast_next=(), tm=1024, tf=512):
    t, d = x.shape
    f = w_gate.shape[1]
    ni, nj = t // tm, f // tf
    cast_specs = []
    if cast_next:
        assert d % ni == 0 and f % (ni * nj) == 0
        gate_up = pl.BlockSpec((d // ni, tf), lambda i, j: (i, j))
        down = pl.BlockSpec((f // (ni * nj), d), lambda i, j: (i * nj + j, 0))
        cast_specs = [gate_up, gate_up, down]
    outs = pl.pallas_call(
        _ffn_kernel,
        grid=(ni, nj),
        in_specs=[
            pl.BlockSpec((tm, d), lambda i, j: (i, 0)),
            pl.BlockSpec((1, d), lambda i, j: (0, 0)),
            pl.BlockSpec((d, tf), lambda i, j: (0, j)),
            pl.BlockSpec((d, tf), lambda i, j: (0, j)),
            pl.BlockSpec((tf, d), lambda i, j: (j, 0)),
        ] + cast_specs,
        out_specs=[pl.BlockSpec((tm, d), lambda i, j: (i, 0))] + cast_specs,
        out_shape=[jax.ShapeDtypeStruct((t, d), F32)] + [jax.ShapeDtypeStruct(w.shape, BF16) for w in cast_next],
        scratch_shapes=[pltpu.VMEM((tm, d), BF16)],
        compiler_params=_params("parallel", "arbitrary"),
        name="ffn",
    )(x, gain.reshape(1, d), w_gate, w_up, w_down, *cast_next)
    return outs[0], outs[1:]


def _in_proj_kernel(x_ref, g_ref, w_ref, qg_ref, kg_ref, u_ref, o_ref, y_ref):
    h = _rms(x_ref[...], g_ref[...]).astype(BF16)
    section = lambda j: _dot(h, w_ref[:, j * SB_WIDTH:(j + 1) * SB_WIDTH])

    def head_norm(y, gain, scale, col0):
        for hd in range(SB_HEADS):
            sl = slice(hd * SB_HEAD_DIM, (hd + 1) * SB_HEAD_DIM)
            o_ref[:, col0 + sl.start:col0 + sl.stop] = (_rms(y[:, sl], gain) * scale).astype(BF16)

    y = section(0)
    for s in range(y_ref.shape[0]):
        lanes = slice(s * LANES, (s + 1) * LANES)
        y_ref[s] = y[:, lanes]
        for t in range(SSM_CHUNK):
            u_ref[t, :, lanes] = y_ref[s, pl.ds(t, u_ref.shape[1], stride=SSM_CHUNK), :].astype(BF16)
    head_norm(section(1), qg_ref[...], SB_HEAD_DIM ** -0.5 * LOG2_E, 0)
    head_norm(section(2), kg_ref[...], 1.0, SB_WIDTH)
    o_ref[:, 2 * SB_WIDTH:] = section(3).astype(BF16)


def _in_proj(x, gain, w_in, q_gain, k_gain, *, n_batch, tm=512):
    t, d = x.shape
    n = w_in.shape[1]
    n_chunks = t // n_batch // SSM_CHUNK
    tiles = t // n_batch // tm
    const = lambda shape: pl.BlockSpec(shape, lambda i: (0, 0), pipeline_mode=pl.Buffered(1))
    return pl.pallas_call(
        _in_proj_kernel,
        grid=(t // tm,),
        in_specs=[pl.BlockSpec((tm, d), lambda i: (i, 0)), const((1, d)), const((d, n)),
                  const((1, SB_HEAD_DIM)), const((1, SB_HEAD_DIM))],
        out_specs=[pl.BlockSpec((SSM_CHUNK, None, tm // SSM_CHUNK, SSM_WIDTH),
                                lambda i: (0, i // tiles, i % tiles, 0)),
                   pl.BlockSpec((tm, n - SSM_WIDTH), lambda i: (i, 0))],
        out_shape=[jax.ShapeDtypeStruct((SSM_CHUNK, n_batch, n_chunks, SSM_WIDTH), BF16),
                   jax.ShapeDtypeStruct((t, n - SSM_WIDTH), BF16)],
        scratch_shapes=[pltpu.VMEM((SSM_WIDTH // LANES, tm, LANES), F32)],
        compiler_params=_params("parallel"),
        name="in_proj",
    )(x, gain.reshape(1, d), w_in, q_gain.reshape(1, -1), k_gain.reshape(1, -1))


def _ssm_prep_kernel(*refs):
    for gi in range(refs[0].shape[0]):
        _ssm_prep_group(*[r.at[gi] for r in refs])


def _ssm_prep_group(lre_ref, lim_ref, ldt_ref, bre_ref, bim_ref, cre_ref, cim_ref, d_ref,
                    k_ref, wsr_ref, wsi_ref, cor_ref, coi_ref, zr_ref, zi_ref):
    cdim, hdim, pdim = SSM_CHUNK, SSM_GROUP, SSM_STATE
    lre = jnp.minimum(lre_ref[...], LAMBDA_RE_MAX)
    lim = lim_ref[...]
    dt = jnp.exp(ldt_ref[...])
    a = lre * dt
    th = lim * dt

    def zpow(steps):
        mag = jnp.exp(steps * a)
        return mag * jnp.cos(steps * th), mag * jnp.sin(steps * th)

    z1r, z1i = zpow(jnp.ones((1, 1), F32))
    den = lre * lre + lim * lim
    cfr = ((z1r - 1.0) * lre + z1i * lim) / den
    cfi = (z1i * lre - (z1r - 1.0) * lim) / den
    btr, bti = bre_ref[...], bim_ref[...]
    bbr = cfr * btr - cfi * bti
    bbi = cfr * bti + cfi * btr
    cr, ci = cre_ref[...], cim_ref[...]

    steps = lax.broadcasted_iota(jnp.int32, (cdim, 1), 0).astype(F32)

    def outer(zr, zi, mr, mi):
        zr, zi = zr[:, None, :], zi[:, None, :]
        mr, mi = mr[None, :, :], mi[None, :, :]
        return ((zr * mr - zi * mi).reshape(cdim * hdim, pdim),
                (zr * mi + zi * mr).reshape(cdim * hdim, pdim))

    zr, zi = zpow(steps)
    czr, czi = outer(zr, zi, cr, ci)
    kk = (_dot_nt(czr, bbr, precision=lax.Precision.HIGHEST)
          - _dot_nt(czi, bbi, precision=lax.Precision.HIGHEST))
    row = lax.broadcasted_iota(jnp.int32, kk.shape, 0)
    col = lax.broadcasted_iota(jnp.int32, kk.shape, 1)
    k_ref[...] = kk + jnp.where(row == col, d_ref[...], 0.0)

    zr, zi = zpow((cdim - 1.0) - steps)
    wsr_ref[...], wsi_ref[...] = outer(zr, zi, bbr, bbi)

    zr, zi = zpow(steps + 1.0)
    cor, coi = outer(zr, zi, cr, ci)
    cor_ref[...] = cor
    coi_ref[...] = -coi

    zr_ref[...], zi_ref[...] = zpow(jnp.full((1, 1), float(cdim), F32))


def _ssm_prep(lam_re, lam_im, b_re, b_im, c_re, c_im, log_dt, d_skip):
    g, p, h, r = SSM_GROUPS, SSM_STATE, SSM_GROUP, SSM_ROW

    def per_group(*shape):
        return pl.BlockSpec((SLAB_GROUPS,) + shape, lambda i: (i,) + (0,) * len(shape))

    outs = pl.pallas_call(
        _ssm_prep_kernel,
        grid=(g // SLAB_GROUPS,),
        in_specs=[per_group(1, p), per_group(1, p), per_group(1, 1), per_group(h, p), per_group(h, p),
                  per_group(h, p), per_group(h, p), per_group(1, h)],
        out_specs=[per_group(r, h), per_group(r, p), per_group(r, p), per_group(r, p), per_group(r, p),
                   per_group(1, p), per_group(1, p)],
        out_shape=[jax.ShapeDtypeStruct((g, r, h), F32)] + [jax.ShapeDtypeStruct((g, r, p), F32)] * 4
                  + [jax.ShapeDtypeStruct((g, 1, p), F32)] * 2,
        compiler_params=_params("parallel"),
        name="ssm_prep",
    )(lam_re.reshape(g, 1, p), lam_im.reshape(g, 1, p), log_dt.reshape(g, 1, 1),
      b_re.transpose(0, 2, 1), b_im.transpose(0, 2, 1), c_re, c_im, d_skip.reshape(g, 1, h))
    kk, wsr, wsi, cor, coi, zr, zi = outs

    c, ns, gl = SSM_CHUNK, SSM_SLABS, SLAB_GROUPS
    same = jnp.eye(gl, dtype=bool)
    kt = kk.reshape(ns, gl, c, h, h).transpose(0, 2, 1, 4, 3).astype(BF16)
    lag_blocks = jnp.where(same[None, None, :, None, :, None], kt[:, :, :, :, None, :], 0)
    lag_blocks = lag_blocks.reshape(ns, c, gl * h, gl * h)

    def per_step(m):
        m = m.reshape(ns, gl, c, h, p).transpose(0, 2, 1, 3, 4).reshape(ns, c, gl * h, p).astype(BF16)
        return jnp.concatenate([m, m], axis=3)

    st_blocks = jnp.stack([per_step(wsr), per_step(wsi)], axis=1)
    out_blocks = jnp.stack([per_step(cor), per_step(coi)], axis=1)
    return lag_blocks, st_blocks, out_blocks, zr.reshape(ns, 1, gl * p), zi.reshape(ns, 1, gl * p)


def _ssm_scan_kernel(u_ref, lag_ref, stb_ref, outb_ref, zr_ref, zi_ref, y_ref,
                     toep_ref, wst_ref, wout_ref, ye_ref, sp_ref, st_ref):
    c_dim, n_batch, n_chunks, _ = u_ref.shape
    n_s = st_ref.shape[1] // 2

    @pl.when(pl.program_id(1) == 0)
    def _():
        st_ref[...] = jnp.zeros(st_ref.shape, F32)
        row_group = lax.broadcasted_iota(jnp.int32, (LANES, n_s), 0) // SSM_GROUP
        col_group = lax.broadcasted_iota(jnp.int32, (LANES, n_s), 1) // SSM_STATE
        own_group = row_group == col_group
        for tau in range(c_dim):
            rows = slice(tau * LANES, (tau + 1) * LANES)
            for t in range(c_dim):
                blk = lag_ref[t - tau] if t >= tau else jnp.zeros((LANES, LANES), BF16)
                toep_ref[rows, t * LANES:(t + 1) * LANES] = blk
            for part in range(2):
                cols = slice(part * n_s, (part + 1) * n_s)
                tile = lambda blk: jnp.concatenate([blk] * (n_s // LANES), axis=1)
                wst_ref[rows, cols] = jnp.where(own_group, tile(stb_ref[part, tau]), 0).astype(BF16)
                wout_ref[rows, cols] = jnp.where(own_group, tile(outb_ref[part, tau]), 0).astype(BF16)

    lhs = jnp.concatenate([u_ref[t].reshape(n_batch * n_chunks, LANES) for t in range(c_dim)], axis=1)
    wide = 2 * LANES
    y_intra = [_dot(lhs[:, :(tb + 1) * wide], toep_ref[:(tb + 1) * wide, tb * wide:(tb + 1) * wide])
               for tb in range(c_dim * LANES // wide)]
    ends = _dot(lhs, wst_ref[...])
    n_v = n_s // LANES
    for k in range(2 * n_v):
        ye_ref[k] = ends[:, k * LANES:(k + 1) * LANES]
    lane_slab = lambda ref, k: ref[:, k * LANES:(k + 1) * LANES]
    zr = [jnp.broadcast_to(lane_slab(zr_ref, k), (n_batch, LANES)) for k in range(n_v)]
    zi = [jnp.broadcast_to(lane_slab(zi_ref, k), (n_batch, LANES)) for k in range(n_v)]

    def step(c, state):
        rows = pl.ds(c, n_batch, stride=n_chunks)
        new = []
        for k in range(n_v):
            sr, si = state[k], state[n_v + k]
            sp_ref[k, rows, :] = sr
            sp_ref[n_v + k, rows, :] = si
            new.append((zr[k] * sr - zi[k] * si + ye_ref[k, rows, :],
                        zr[k] * si + zi[k] * sr + ye_ref[n_v + k, rows, :]))
        return tuple(s[0] for s in new) + tuple(s[1] for s in new)

    state = lax.fori_loop(0, n_chunks, step, tuple(lane_slab(st_ref, k) for k in range(2 * n_v)), unroll=4)
    for k in range(2 * n_v):
        st_ref[:, k * LANES:(k + 1) * LANES] = state[k]
    s_prev = jnp.concatenate([sp_ref[k] for k in range(2 * n_v)], axis=1)
    y_carry = _dot_nt(s_prev.astype(BF16), wout_ref[...])
    for t in range(c_dim):
        lo = t * LANES % wide
        y_t = y_intra[t * LANES // wide][:, lo:lo + LANES] + y_carry[:, t * LANES:(t + 1) * LANES]
        for b in range(n_batch):
            y_ref[b, pl.ds(t, n_chunks, stride=c_dim), :] = y_t[b * n_chunks:(b + 1) * n_chunks]


def _ssm_scan(u_t, lag_blocks, st_blocks, out_blocks, zr, zi, *, chunk_tile=64):
    c, n_batch, n_chunks, _ = u_t.shape
    n_s = zr.shape[2]
    rows = n_batch * chunk_tile

    def per_slab(a):
        zeros = (0,) * (a.ndim - 1)
        return pl.BlockSpec((None,) + a.shape[1:], lambda s, j: (s,) + zeros, pipeline_mode=pl.Buffered(1))

    return pl.pallas_call(
        _ssm_scan_kernel,
        grid=(SSM_SLABS, n_chunks // chunk_tile),
        in_specs=[pl.BlockSpec((c, n_batch, chunk_tile, LANES), lambda s, j: (0, 0, j, s)),
                  per_slab(lag_blocks), per_slab(st_blocks), per_slab(out_blocks), per_slab(zr), per_slab(zi)],
        out_specs=pl.BlockSpec((n_batch, chunk_tile * c, LANES), lambda s, j: (0, j, s)),
        out_shape=jax.ShapeDtypeStruct((n_batch, n_chunks * c, SSM_WIDTH), F32),
        scratch_shapes=[pltpu.VMEM((c * LANES, c * LANES), BF16),
                        pltpu.VMEM((c * LANES, 2 * n_s), BF16), pltpu.VMEM((c * LANES, 2 * n_s), BF16),
                        pltpu.VMEM((2 * n_s // LANES, rows, LANES), F32),
                        pltpu.VMEM((2 * n_s // LANES, rows, LANES), F32), pltpu.VMEM((n_batch, 2 * n_s), F32)],
        compiler_params=_params("parallel", "arbitrary"),
        name="ssm_scan",
    )(u_t, lag_blocks, st_blocks, out_blocks, zr, zi)


def _ssm_glu_kernel(y_ref, w_ref, b_ref, g_ref, o_ref):
    y = y_ref[...]
    z = 0.5 * y * (1.0 + jnp.tanh(math.sqrt(2.0 / math.pi) * (y + 0.044715 * (y * y * y))))
    out = z * jax.nn.sigmoid(_dot(z.astype(BF16), w_ref[...]) + b_ref[...])
    o_ref[...] = _rms(out, g_ref[...]).astype(BF16)


def _ssm_glu(y, w_glu, b_glu, gain, *, tm=1024):
    t, w = y.shape
    return pl.pallas_call(
        _ssm_glu_kernel,
        grid=(t // tm,),
        in_specs=[pl.BlockSpec((tm, w), lambda i: (i, 0)), pl.BlockSpec((w, w), lambda i: (0, 0)),
                  pl.BlockSpec((1, w), lambda i: (0, 0)), pl.BlockSpec((1, w), lambda i: (0, 0))],
        out_specs=pl.BlockSpec((tm, w), lambda i: (i, 0)),
        out_shape=jax.ShapeDtypeStruct((t, w), BF16),
        compiler_params=_params("parallel"),
        name="ssm_glu",
    )(y, w_glu, b_glu.reshape(1, w), gain.reshape(1, w))


def _sb_attn_kernel(q_ref, k_ref, v_ref, o_ref, run_ref):
    seq = q_ref.shape[0]
    n_sub = Q_SPAN // Q_TILE
    diag_blocks = Q_SPAN // K_TILE
    span_blocks = K_SPAN // K_TILE
    s_idx = lax.broadcasted_iota(jnp.int32, (K_TILE, K_TILE), 0)
    j_idx = lax.broadcasted_iota(jnp.int32, (K_TILE, K_TILE), 1)
    later = (s_idx >= j_idx).astype(BF16)
    later2 = jnp.concatenate([later, later], axis=0)
    row =lax.broadcasted_iota(jnp.int32, (Q_TILE, K_TILE), 0)
    col = lax.broadcasted_iota(jnp.int32, (Q_TILE, K_TILE), 1)

    def sweep(base, blocks):
        n, depth = len(blocks), 6
        z, mid, cs = {}, {}, {}
        for step in range(n + depth - 1):
            i = step
            if i < n:
                r0, nr, kb, _ = blocks[i]
                keys = pl.ds(pl.multiple_of(kb * K_TILE, K_TILE), K_TILE)
                z[i] = _dot_nt(q_ref[pl.ds(base + r0, nr), :], k_ref[keys, :])
            i = step - 2
            if 0 <= i < n:
                valid = blocks[i][3]
                zi = z.pop(i)
                softplus = jnp.maximum(zi, 0.0) + jnp.log(1.0 + jnp.exp2(-jnp.abs(zi))) * LOG2_E
                drop = softplus if valid is None else jnp.where(valid, softplus, 0.0)
                hi = drop.astype(BF16)
                lo = (drop - hi.astype(F32)).astype(BF16)
                mid[i] = (zi, jnp.concatenate([hi, lo], axis=1))
            i = step - 3
            if 0 <= i < n:
                cs[i] = _dot(mid[i][1], later2)
            i = step - 5
            if 0 <= i < n:
                r0, nr, kb, valid = blocks[i]
                keys = pl.ds(pl.multiple_of(kb * K_TILE, K_TILE), K_TILE)
                zi, _ = mid.pop(i)
                csi = cs.pop(i)
                total = csi[:, :1]
                run = run_ref[r0:r0 + nr, :]
                w = jnp.exp2(zi - (run + csi))
                if valid is not None:
                    w = jnp.where(valid, w, 0.0)
                o_ref[pl.ds(base + r0, nr), :] += _dot(w.astype(BF16), v_ref[keys, :])
                run_ref[r0:r0 + nr, :] = run + total

    def q_span(qs, _):
        base = pl.multiple_of(qs * Q_SPAN, Q_SPAN)
        kb_diag = qs * diag_blocks
        o_ref[pl.ds(base, Q_SPAN), :] = jnp.zeros((Q_SPAN, SB_HEAD_DIM), F32)
        run_ref[...] = jnp.zeros(run_ref.shape, F32)
        diag = []
        for d in range(diag_blocks):
            for a in range(n_sub):
                top = (a * Q_TILE) // K_TILE
                if d == 0:
                    diag.append((a * Q_TILE, Q_TILE, kb_diag + top, top * K_TILE + col < a * Q_TILE + row))
                elif top - d >= 0:
                    diag.append((a * Q_TILE, Q_TILE, kb_diag + top - d, None))
        sweep(base, diag)

        def below(n, _):
            kb_hi = kb_diag - 1 - n * span_blocks
            sweep(base, [(a * Q_TILE, Q_TILE, kb_hi - d, None) for d in range(span_blocks) for a in range(n_sub)])
            return 0

        lax.fori_loop(0, qs * (Q_SPAN // K_SPAN), below, 0)
        return 0

    lax.fori_loop(0, seq // Q_SPAN, q_span, 0)


def _sb_attn(qkv, *, n_batch, seq):
    t = qkv.shape[0]
    blk = lambda off: pl.BlockSpec((seq, SB_HEAD_DIM), lambda b, h: (b, off + h))
    return pl.pallas_call(
        _sb_attn_kernel,
        grid=(n_batch, SB_HEADS),
        in_specs=[blk(0), blk(SB_HEADS), blk(2 * SB_HEADS)],
        out_specs=blk(0),
        out_shape=jax.ShapeDtypeStruct((t, SB_WIDTH), F32),
        scratch_shapes=[pltpu.VMEM((Q_SPAN, 1), F32)],
        compiler_params=_params("parallel", "parallel"),
        name="sb_attn",
    )(qkv, qkv, qkv)


def _out_proj_kernel(x_ref, ms_ref, sb_ref, g_ref, w_ref, o_ref):
    sb = _rms(sb_ref[...], g_ref[...]).astype(BF16)
    o_ref[...] = (x_ref[...] + _dot(ms_ref[...], w_ref[:SSM_WIDTH, :]) + _dot(sb, w_ref[SSM_WIDTH:, :]))


def _out_proj(x, mixed_ssm, y_sb, sb_gain, w_out, *, tm=512):
    t, d = x.shape
    row = lambda w: pl.BlockSpec((tm, w), lambda i: (i, 0))
    return pl.pallas_call(
        _out_proj_kernel,
        grid=(t // tm,),
        in_specs=[row(d), row(SSM_WIDTH), row(SB_WIDTH), pl.BlockSpec((1, SB_WIDTH), lambda i: (0, 0)),
                  pl.BlockSpec(w_out.shape, lambda i: (0, 0))],
        out_specs=row(d),
        out_shape=jax.ShapeDtypeStruct((t, d), F32),
        compiler_params=_params("parallel"),
        name="out_proj",
    )(x, mixed_ssm, y_sb, sb_gain.reshape(1, -1), w_out)


def _ple_kernel(x_ref, p_ref, g_ref, wg_ref, wp_ref, pg_ref, o_ref):
    x = x_ref[...]
    gate = jax.nn.sigmoid(_dot(_rms(x, g_ref[...]).astype(BF16), wg_ref[...]))
    e = _dot(p_ref[...].astype(BF16), wp_ref[...]) * gate
    o_ref[...] = x + _rms(e, pg_ref[...])


def _ple(x, p, gain, w_gate, w_proj, post_gain, *, tm=512):
    t, d = x.shape
    row = lambda w: pl.BlockSpec((tm, w), lambda i: (i, 0))
    vec = pl.BlockSpec((1, d), lambda i: (0, 0))
    return pl.pallas_call(
        _ple_kernel,
        grid=(t // tm,),
        in_specs=[row(d), row(p.shape[1]), vec, pl.BlockSpec(w_gate.shape, lambda i: (0, 0)),
                  pl.BlockSpec(w_proj.shape, lambda i: (0, 0)), vec],
        out_specs=row(d),
        out_shape=jax.ShapeDtypeStruct((t, d), F32),
        compiler_params=_params("parallel"),
        name="ple",
    )(x, p, gain.reshape(1, d), w_gate, w_proj, post_gain.reshape(1, d))


def _s5_mixer(u, n_batch, seq, lam_re, lam_im, b_re, b_im, c_re, c_im, log_dt, d_skip, w_glu, b_glu, gain):
    mats = _ssm_prep(lam_re, lam_im, b_re, b_im, c_re, c_im, log_dt, d_skip)
    y = _ssm_scan(u, *mats)
    return _ssm_glu(y.reshape(n_batch * seq, SSM_WIDTH), w_glu.astype(BF16), b_glu, gain)


def kernel(x, p, ffn1_norm, ffn1_w_gate, ffn1_w_up, ffn1_w_down, mix_norm, w_in, ssm_lambda_re, ssm_lambda_im, ssm_b_re, ssm_b_im, ssm_c_re, ssm_c_im, ssm_log_dt, ssm_d, ssm_w_glu, ssm_b_glu, q_norm, k_norm, out_norm_ssm, out_norm_sb, w_out, ffn2_norm, ffn2_w_gate, ffn2_w_up, ffn2_w_down, ple_norm, w_ple_gate, w_ple_proj, ple_post_norm):
    n_batch, seq, d = x.shape
    xt = x.reshape(n_batch * seq, d)
    for i in range(p.shape[0]):
        xt, ffn2_bf16 = _ffn(xt, ffn1_norm[i], ffn1_w_gate[i].astype(BF16), ffn1_w_up[i].astype(BF16),
                             ffn1_w_down[i].astype(BF16), cast_next=(ffn2_w_gate[i], ffn2_w_up[i], ffn2_w_down[i]))
        u, qkv = _in_proj(xt, mix_norm[i], w_in[i].astype(BF16), q_norm[i], k_norm[i], n_batch=n_batch)
        mixed_ssm = _s5_mixer(u, n_batch, seq, ssm_lambda_re[i], ssm_lambda_im[i],
                              ssm_b_re[i], ssm_b_im[i], ssm_c_re[i], ssm_c_im[i], ssm_log_dt[i], ssm_d[i],
                              ssm_w_glu[i], ssm_b_glu[i], out_norm_ssm[i])
        y_sb = _sb_attn(qkv, n_batch=n_batch, seq=seq)
        xt = _out_proj(xt, mixed_ssm, y_sb, out_norm_sb[i], w_out[i].astype(BF16))
        xt, _ = _ffn(xt, ffn2_norm[i], *ffn2_bf16)
        xt = _ple(xt, p[i].reshape(n_batch * seq, -1), ple_norm[i], w_ple_gate[i].astype(BF16),
                  w_ple_proj[i].astype(BF16), ple_post_norm[i])
    return xt.reshape(n_batch, seq, d)
```
